```python
import math
import jax, jax.numpy as jnp
from jax import lax
import numpy as np

D_MODEL = 2048
BATCH = 2
SEQ = 16384
DEPTH = 1
DEC_BATCH = 2
DEC_SEQ = 4096
PAST_LEN = 128

RET_HEADS = 8
RET_DK = 256
RET_DV = 512
RET_CHUNK = 128
ROPE_BASE = 10000.0
GDN_QK_HEADS = 16
GDN_V_HEADS = 32
GDN_DK = 128
GDN_DV = 128
GDN_CONV = 5
GDN_CHUNK = 64
MEM_LEN = 256
XA_HEADS = 4
XA_DH = D_MODEL // XA_HEADS
N_EXPERTS = 32
TOP_K = 4
D_FF = 2048
SWIGLU_LIMIT = 7.0
SWIGLU_ALPHA = 1.702
MOE_BLOCK = 512
NORM_EPS = 1e-6

N_BRANCH = 2
RET_QK = RET_HEADS * RET_DK
RET_V = RET_HEADS * RET_DV
GDN_QK = GDN_QK_HEADS * GDN_DK
GDN_V = GDN_V_HEADS * GDN_DV
BRANCH_W = RET_V
GDN_CONV_CH = 2 * GDN_QK + GDN_V
IN_SPLITS = (RET_QK, RET_QK, RET_V, RET_V, GDN_QK, GDN_QK, GDN_V, GDN_V, 2 * GDN_V_HEADS, 2 * GDN_V_HEADS, N_BRANCH * D_MODEL)
D_IN = 2 * RET_QK + 2 * RET_V + 2 * GDN_QK + 2 * GDN_V + 4 * GDN_V_HEADS + N_BRANCH * D_MODEL

kernel_name = 'retention_gdn_moe_xattn_encoder'


def _rmsnorm(x, gain):
    xf = x.astype(jnp.float32)
    y = xf * lax.rsqrt(jnp.mean(xf * xf, axis=-1, keepdims=True) + NORM_EPS)
    return (y * gain.astype(jnp.float32)).astype(x.dtype)


def _l2norm(x):
    xf = x.astype(jnp.float32)
    return xf * lax.rsqrt(jnp.sum(xf * xf, axis=-1, keepdims=True) + NORM_EPS)


def _rotary(x):
    S, dh = x.shape[1], x.shape[-1]
    inv_freq = ROPE_BASE ** (-jnp.arange(0, dh, 2, dtype=jnp.float32) / dh)
    ang = jnp.arange(S, dtype=jnp.float32)[:, None] * inv_freq[None, :]
    cos = jnp.cos(ang)[None, :, None, :]
    sin = jnp.sin(ang)[None, :, None, :]
    xf = x.astype(jnp.float32)
    x1, x2 = xf[..., : dh // 2], xf[..., dh // 2:]
    return jnp.concatenate([x1 * cos - x2 * sin, x1 * sin + x2 * cos], axis=-1).astype(x.dtype)


def _retention_dir(q, k, v, log_gamma, include_diag):
    B, S, H, dk = q.shape
    dv = v.shape[-1]
    C = RET_CHUNK
    N = S // C

    def chunks(t):
        return jnp.moveaxis(t.astype(jnp.float32).reshape(B, N, C, H, t.shape[-1]), 1, 0)

    lg = log_gamma.astype(jnp.float32)
    idx = jnp.arange(C, dtype=jnp.float32)
    diff = idx[:, None] - idx[None, :]
    mask = (diff >= 0) if include_diag else (diff > 0)
    dmat = jnp.where(mask[None], jnp.exp(lg[:, None, None] * jnp.where(mask, diff, 0.0)[None]), 0.0)
    q_dec = jnp.exp(lg[None, :] * (idx[:, None] + 1.0))
    k_dec = jnp.exp(lg[None, :] * (C - 1.0 - idx[:, None]))
    c_dec = jnp.exp(lg * C)

    def step(R, xs):
        qc, kc, vc = xs
        scores = jnp.einsum('bihd,bjhd->bhij', qc, kc) * dmat[None]
        o = jnp.einsum('bhij,bjhv->bihv', scores, vc)
        o = o + jnp.einsum('bihd,bhdv->bihv', qc * q_dec[None, :, :, None], R)
        R = R * c_dec[None, :, None, None] + jnp.einsum('bjhd,bjhv->bhdv', kc * k_dec[None, :, :, None], vc)
        return R, o

    R0 = jnp.zeros((B, H, dk, dv), jnp.float32)
    _, o = lax.scan(step, R0, (chunks(q), chunks(k), chunks(v)))
    return jnp.moveaxis(o, 0, 1).reshape(B, S, H, dv)


def _retention(q, k, v, g, log_decay, gn_w):
    B, S, _ = q.shape
    q = _rotary(q.reshape(B, S, RET_HEADS, RET_DK))
    k = _rotary(k.reshape(B, S, RET_HEADS, RET_DK)) * (RET_DK ** -0.5)
    v = v.reshape(B, S, RET_HEADS, RET_DV)
    lg = -jnp.abs(log_decay.astype(jnp.float32))
    fwd = _retention_dir(q, k, v, lg[0], True)
    bwd = jnp.flip(_retention_dir(jnp.flip(q, 1), jnp.flip(k, 1), jnp.flip(v, 1), lg[1], False), 1)
    o = fwd + bwd
    mu = jnp.mean(o, axis=-1, keepdims=True)
    var = jnp.mean(jnp.square(o - mu), axis=-1, keepdims=True)
    o = ((o - mu) * lax.rsqrt(var + NORM_EPS)).reshape(B, S, RET_V) * gn_w.astype(jnp.float32)
    return (o * jax.nn.silu(g.astype(jnp.float32))).astype(g.dtype)


def _centred_conv(x, w):
    W = w.shape[0]
    return lax.conv_general_dilated(x, w[:, None, :], window_strides=(1,), padding=[(W // 2, W // 2)],
                                    dimension_numbers=('NWC', 'WIO', 'NWC'), feature_group_count=x.shape[-1])


def _gdn_dir(q, k, v, beta, g):
    B, S, H, dk = q.shape
    dv = v.shape[-1]
    C = GDN_CHUNK
    N = S // C

    def chunks4(t):
        return t.astype(jnp.float32).reshape(B, N, C, H, t.shape[-1]).transpose(1, 0, 3, 2, 4)

    def chunks3(t):
        return t.astype(jnp.float32).reshape(B, N, C, H).transpose(1, 0, 3, 2)

    idx = jnp.arange(C)
    tri = idx[:, None] >= idx[None, :]
    strict = idx[:, None] > idx[None, :]
    eye = jnp.eye(C, dtype=jnp.float32)

    def step(St, xs):
        qc, kc, vc, bc, gc = xs
        dec = jnp.cumsum(gc, axis=-1)
        diff = dec[..., :, None] - dec[..., None, :]
        L = jnp.where(tri, jnp.exp(jnp.where(tri, diff, 0.0)), 0.0)
        kb = kc * bc[..., None]
        A = jnp.where(strict, jnp.einsum('bhid,bhjd->bhij', kb, kc) * L, 0.0)
        rhs = jnp.concatenate([vc * bc[..., None], kb * jnp.exp(dec)[..., None]], axis=-1)
        sol = lax.linalg.triangular_solve(eye + A, rhs, left_side=True, lower=True, unit_diagonal=True)
        u, w = sol[..., :dv], sol[..., dv:]
        v_new = u - jnp.einsum('bhik,bhkv->bhiv', w, St)
        attn = jnp.einsum('bhid,bhjd->bhij', qc, kc) * L
        o = jnp.einsum('bhid,bhdv->bhiv', qc * jnp.exp(dec)[..., None], St) + jnp.einsum('bhij,bhjv->bhiv', attn, v_new)
        k_tail = kc * jnp.exp(dec[..., -1:] - dec)[..., None]
        St = St * jnp.exp(dec[..., -1])[..., None, None] + jnp.einsum('bhjd,bhjv->bhdv', k_tail, v_new)
        return St, o

    S0 = jnp.zeros((B, H, dk, dv), jnp.float32)
    _, o = lax.scan(step, S0, (chunks4(q), chunks4(k), chunks4(v), chunks3(beta), chunks3(g)))
    return o.transpose(1, 0, 3, 2, 4).reshape(B, S, H, dv)


def _gated_deltanet(q, k, v, z, b, a, conv_w, a_log, dt_bias, norm_w):
    B, S, _ = q.shape
    qkv = jax.nn.silu(_centred_conv(jnp.concatenate([q, k, v], axis=-1), conv_w))
    q, k, v = qkv[..., :GDN_QK], qkv[..., GDN_QK:2 * GDN_QK], qkv[..., 2 * GDN_QK:]
    rep = GDN_V_HEADS // GDN_QK_HEADS
    q = jnp.repeat(_l2norm(q.reshape(B, S, GDN_QK_HEADS, GDN_DK)) * (GDN_DK ** -0.5), rep, axis=2)
    k = jnp.repeat(_l2norm(k.reshape(B, S, GDN_QK_HEADS, GDN_DK)), rep, axis=2)
    v = v.reshape(B, S, GDN_V_HEADS, GDN_DV)
    beta = jax.nn.sigmoid(b.astype(jnp.float32)).reshape(B, S, 2, GDN_V_HEADS)
    g = -jnp.exp(a_log.astype(jnp.float32)) * jax.nn.softplus(a.astype(jnp.float32).reshape(B, S, 2, GDN_V_HEADS) + dt_bias.astype(jnp.float32))
    fwd = _gdn_dir(q, k, v, beta[:, :, 0], g[:, :, 0])
    bwd = jnp.flip(_gdn_dir(jnp.flip(q, 1), jnp.flip(k, 1), jnp.flip(v, 1), jnp.flip(beta[:, :, 1], 1), jnp.flip(g[:, :, 1], 1)), 1)
    o = fwd + bwd
    o = o * lax.rsqrt(jnp.mean(o * o, axis=-1, keepdims=True) + NORM_EPS) * norm_w.astype(jnp.float32)
    return (o.reshape(B, S, GDN_V) * jax.nn.silu(z.astype(jnp.float32))).astype(z.dtype)


def _mixer(h, w_in, ret_log_decay, ret_gn_w, gdn_conv_w, gdn_a_log, gdn_dt_bias, gdn_norm_w, w_branch, w_out):
    B, S, _ = h.shape
    proj = h @ w_in
    splits = np.cumsum(np.array(IN_SPLITS))[:-1].tolist()
    rq, rk, rv, rg, gq, gk, gv, gz, gb, ga, gate_logits = jnp.split(proj, splits, axis=-1)
    ret = _retention(rq, rk, rv, rg, ret_log_decay, ret_gn_w)
    gdn = _gated_deltanet(gq, gk, gv, gz, gb, ga, gdn_conv_w, gdn_a_log, gdn_dt_bias, gdn_norm_w)
    branches = jnp.einsum('bsnv,nvd->bsnd', jnp.stack([ret, gdn], axis=2), w_branch)
    gates = jax.nn.sigmoid(gate_logits.reshape(B, S, N_BRANCH, D_MODEL))
    return jnp.sum(gates * branches, axis=2) @ w_out


def _cross_attention(h, m, w_q, w_kv, w_o):
    B, S, D = h.shape
    M = m.shape[1]
    q = (h @ w_q).reshape(B, S, XA_HEADS, XA_DH)
    kv = (m @ w_kv).reshape(B, M, 2, XA_HEADS, XA_DH)
    s = jnp.einsum('bshd,bmhd->bhsm', q, kv[:, :, 0]).astype(jnp.float32) * (XA_DH ** -0.5)
    p = jax.nn.softmax(s, axis=-1).astype(h.dtype)
    o = jnp.einsum('bhsm,bmhd->bshd', p, kv[:, :, 1]).reshape(B, S, D)
    return o @ w_o


def _moe(h, w_router, b_router, w_gate_up, b_gate_up, w_down, b_down):
    B, S, D = h.shape
    T = B * S
    A = T * TOP_K
    NB = A // MOE_BLOCK + N_EXPERTS
    xt = h.reshape(T, D)
    logits = (xt @ w_router + b_router).astype(jnp.float32)
    top_val, top_idx = lax.top_k(logits, TOP_K)
    gate = jax.nn.softmax(top_val, axis=-1)
    e_flat = top_idx.reshape(A).astype(jnp.int32)
    g_flat = gate.reshape(A)
    t_flat = jnp.arange(A, dtype=jnp.int32) // TOP_K
    order = jnp.argsort(e_flat)
    e_sorted = e_flat[order]
    counts = jax.ops.segment_sum(jnp.ones((A,), jnp.int32), e_flat, num_segments=N_EXPERTS)
    padded = (counts + MOE_BLOCK - 1) // MOE_BLOCK * MOE_BLOCK
    pad_end = jnp.cumsum(padded)
    pad_start = pad_end - padded
    start = jnp.cumsum(counts) - counts
    dest = pad_start[e_sorted] + jnp.arange(A, dtype=jnp.int32) - start[e_sorted]
    slot_tok = jnp.full((NB * MOE_BLOCK,), T, jnp.int32).at[dest].set(t_flat[order])
    slot_gate = jnp.zeros((NB * MOE_BLOCK,), jnp.float32).at[dest].set(g_flat[order])
    block_e = jnp.minimum(jnp.searchsorted(pad_end, jnp.arange(NB, dtype=jnp.int32) * MOE_BLOCK, side='right'), N_EXPERTS - 1)
    x_pad = jnp.concatenate([xt, jnp.zeros((1, D), xt.dtype)], axis=0)

    def step(y, blk):
        tok, gw, e = blk
        gu = x_pad[tok] @ w_gate_up[e] + b_gate_up[e]
        g_lin = jnp.minimum(gu[:, 0::2], SWIGLU_LIMIT)
        u_lin = jnp.clip(gu[:, 1::2], -SWIGLU_LIMIT, SWIGLU_LIMIT)
        act = g_lin * jax.nn.sigmoid(SWIGLU_ALPHA * g_lin) * (u_lin + 1.0)
        out = act @ w_down[e] + b_down[e]
        return y.at[tok].add(out * gw[:, None].astype(out.dtype)), None

    y, _ = lax.scan(step, jnp.zeros((T + 1, D), h.dtype),
                    (slot_tok.reshape(NB, MOE_BLOCK), slot_gate.reshape(NB, MOE_BLOCK), block_e))
    return y[:T].reshape(B, S, D)


def _encoder(x, mem, ln_mix, w_in, ret_log_decay, ret_gn_w, gdn_conv_w, gdn_a_log, gdn_dt_bias, gdn_norm_w,
             w_branch, w_out, ln_xa, ln_mem, xa_w_q, xa_w_kv, xa_w_o, ln_moe, w_router, b_router,
             w_gate_up, b_gate_up, w_down, b_down, ln_final):
    for l in range(DEPTH):
        x = x + _mixer(_rmsnorm(x, ln_mix[l]), w_in[l], ret_log_decay[l], ret_gn_w[l], gdn_conv_w[l],
                       gdn_a_log[l], gdn_dt_bias[l], gdn_norm_w[l], w_branch[l], w_out[l])
        x = x + _cross_attention(_rmsnorm(x, ln_xa[l]), _rmsnorm(mem, ln_mem[l]), xa_w_q[l], xa_w_kv[l], xa_w_o[l])
        x = x + _moe(_rmsnorm(x, ln_moe[l]), w_router[l], b_router[l], w_gate_up[l], b_gate_up[l], w_down[l], b_down[l])
    return _rmsnorm(x, ln_final)


def setup_inputs(seed: int = 0) -> dict:
    key = jax.random.key(seed)
    ks = jax.random.split(key, 32)
    f32 = jnp.float32
    L = DEPTH

    def nrm(k, shape, scale):
        return jax.random.normal(k, shape, f32) * scale

    def gain(k, shape):
        return 1.0 + 0.02 * jax.random.normal(k, shape, f32)

    base_decay = jnp.log1p(-(2.0 ** (-5.0 - jnp.arange(RET_HEADS, dtype=f32))))
    ret_log_decay = base_decay[None, None, :] * (1.0 + 0.05 * jax.random.normal(ks[6], (L, 2, RET_HEADS), f32))
    gdn_a_log = jnp.log(jax.random.uniform(ks[9], (L, 2, GDN_V_HEADS), f32, 1.0, 16.0))
    dt = jnp.exp(jax.random.uniform(ks[10], (L, 2, GDN_V_HEADS), f32, math.log(1e-3), math.log(1e-1)))
    gdn_dt_bias = dt + jnp.log(-jnp.expm1(-dt))
    return {
        'x_prompt': jax.random.normal(ks[0], (BATCH, SEQ, D_MODEL), f32),
        'x_sample': jax.random.normal(ks[1], (DEC_BATCH, DEC_SEQ, D_MODEL), f32),
        'mem_prompt': jax.random.normal(ks[2], (BATCH, MEM_LEN, D_MODEL), f32),
        'mem_sample': jax.random.normal(ks[3], (DEC_BATCH, MEM_LEN, D_MODEL), f32),
        'ln_mix': gain(ks[4], (L, D_MODEL)),
        'w_in': nrm(ks[5], (L, D_MODEL, D_IN), D_MODEL ** -0.5),
        'ret_log_decay': ret_log_decay,
        'ret_gn_w': gain(ks[7], (L, RET_V)),
        'gdn_conv_w': nrm(ks[8], (L, GDN_CONV, GDN_CONV_CH), GDN_CONV ** -0.5),
        'gdn_a_log': gdn_a_log,
        'gdn_dt_bias': gdn_dt_bias,
        'gdn_norm_w': gain(ks[11], (L, GDN_DV)),
        'w_branch': nrm(ks[12], (L, N_BRANCH, BRANCH_W, D_MODEL), BRANCH_W ** -0.5),
        'w_out': nrm(ks[13], (L, D_MODEL, D_MODEL), D_MODEL ** -0.5),
        'ln_xa': gain(ks[14], (L, D_MODEL)),
        'ln_mem': gain(ks[15], (L, D_MODEL)),
        'xa_w_q': nrm(ks[16], (L, D_MODEL, D_MODEL), D_MODEL ** -0.5),
        'xa_w_kv': nrm(ks[17], (L, D_MODEL, 2 * D_MODEL), D_MODEL ** -0.5),
        'xa_w_o': nrm(ks[18], (L, D_MODEL, D_MODEL), D_MODEL ** -0.5),
        'ln_moe': gain(ks[19], (L, D_MODEL)),
        'w_router': nrm(ks[20], (L, D_MODEL, N_EXPERTS), D_MODEL ** -0.5),
        'b_router': nrm(ks[21], (L, N_EXPERTS), 0.01),
        'w_gate_up': nrm(ks[22], (L, N_EXPERTS, D_MODEL, 2 * D_FF), D_MODEL ** -0.5),
        'b_gate_up': nrm(ks[23], (L, N_EXPERTS, 2 * D_FF), 0.01),
        'w_down': nrm(ks[24], (L, N_EXPERTS, D_FF, D_MODEL), D_FF ** -0.5),
        'b_down': nrm(ks[25], (L, N_EXPERTS, D_MODEL), 0.01),
        'ln_final': gain(ks[26], (D_MODEL,)),
    }


def reference(x_prompt, x_sample, mem_prompt, mem_sample, ln_mix, w_in, ret_log_decay, ret_gn_w, gdn_conv_w,
              gdn_a_log, gdn_dt_bias, gdn_norm_w, w_branch, w_out, ln_xa, ln_mem, xa_w_q, xa_w_kv, xa_w_o,
              ln_moe, w_router, b_router, w_gate_up, b_gate_up, w_down, b_down, ln_final):
    y_prompt = _encoder(x_prompt, mem_prompt, ln_mix, w_in, ret_log_decay, ret_gn_w, gdn_conv_w, gdn_a_log,
                        gdn_dt_bias, gdn_norm_w, w_branch, w_out, ln_xa, ln_mem, xa_w_q, xa_w_kv, xa_w_o,
                        ln_moe, w_router, b_router, w_gate_up, b_gate_up, w_down, b_down, ln_final)
    y_sample = _encoder(x_sample, mem_sample, ln_mix, w_in, ret_log_decay, ret_gn_w, gdn_conv_w, gdn_a_log,
                        gdn_dt_bias, gdn_norm_w, w_branch, w_out, ln_xa, ln_mem, xa_w_q, xa_w_kv, xa_w_o,
                        ln_moe, w_router, b_router, w_gate_up, b_gate_up, w_down, b_down, ln_final)
    return (y_prompt, y_sample)
```

```python
import functools
import math

import jax
import jax.numpy as jnp
from jax import lax
from jax.experimental import pallas as pl
from jax.experimental.pallas import tpu as pltpu

F32 = jnp.float32
BF16 = jnp.bfloat16

D_MODEL = 2048
RET_HEADS = 8
RET_DK = 256
RET_DV = 512
ROPE_BASE = 10000.0
GDN_QK_HEADS = 16
GDN_V_HEADS = 32
GDN_DK = 128
GDN_DV = 128
GDN_CONV = 5
XA_HEADS = 4
XA_DH = D_MODEL // XA_HEADS
N_EXPERTS = 32
TOP_K = 4
D_FF = 2048
SWIGLU_LIMIT = 7.0
SWIGLU_ALPHA = 1.702
MOE_BLOCK = 512
NORM_EPS = 1e-6
N_BRANCH = 2
RET_QK = RET_HEADS * RET_DK
RET_V = RET_HEADS * RET_DV
GDN_QK = GDN_QK_HEADS * GDN_DK
GDN_V = GDN_V_HEADS * GDN_DV

LANES = 128
VMEM_LIMIT = 56 * 1024 * 1024

RET_CHUNK = 256
GDN_CHUNK = 64
GDN_HB = 8


def _cparams(*sem):
    return pltpu.CompilerParams(dimension_semantics=sem, vmem_limit_bytes=VMEM_LIMIT)


def _rmsnorm_body(x_ref, g_ref, o_ref):
    x = x_ref[...].astype(F32)
    ms = jnp.mean(x * x, axis=-1, keepdims=True)
    o_ref[...] = (x * lax.rsqrt(ms + NORM_EPS) * g_ref[...]).astype(o_ref.dtype)


def _rmsnorm(x, gain, out_dtype, tm=512):
    T, D = x.shape
    tm = min(tm, T)
    return pl.pallas_call(
        _rmsnorm_body,
        grid=(T // tm,),
        in_specs=[pl.BlockSpec((tm, D), lambda i: (i, 0)), pl.BlockSpec((1, D), lambda i: (0, 0))],
        out_specs=pl.BlockSpec((tm, D), lambda i: (i, 0)),
        out_shape=jax.ShapeDtypeStruct((T, D), out_dtype),
        compiler_params=_cparams("parallel"),
        name="rmsnorm",
    )(x, gain.reshape(1, D).astype(F32))


def _mm_body(a_ref, b_ref, o_ref):
    o_ref[...] = jnp.dot(a_ref[...], b_ref[...], preferred_element_type=F32).astype(o_ref.dtype)


def _mm_res_body(a_ref, b_ref, r_ref, o_ref):
    o_ref[...] = r_ref[...] + jnp.dot(a_ref[...], b_ref[...], preferred_element_type=F32)


def _matmul(a, b, out_dtype, residual=None, tm=512, tn=1024, name="matmul"):
    M, K = a.shape
    N = b.shape[1]
    tm = min(tm, M)
    tn = min(tn, N)
    in_specs = [pl.BlockSpec((tm, K), lambda i, j: (i, 0)), pl.BlockSpec((K, tn), lambda i, j: (0, j))]
    args = [a, b]
    body = _mm_body
    if residual is not None:
        in_specs.append(pl.BlockSpec((tm, tn), lambda i, j: (i, j)))
        args.append(residual)
        body = _mm_res_body
    return pl.pallas_call(
        body,
        grid=(M // tm, N // tn),
        in_specs=in_specs,
        out_specs=pl.BlockSpec((tm, tn), lambda i, j: (i, j)),
        out_shape=jax.ShapeDtypeStruct((M, N), out_dtype),
        compiler_params=_cparams("parallel", "arbitrary"),
        name=name,
    )(*args)


def _rotary(x, cos, sin):
    half = x.shape[-1] // 2
    x1, x2 = x[:, :half], x[:, half:]
    return jnp.concatenate([x1 * cos - x2 * sin, x1 * sin + x2 * cos], axis=-1)


def _ret_bwd_body(lg_ref, q_ref, k_ref, v_ref, cos_ref, sin_ref, ob_ref, r_ref):
    h = pl.program_id(1)
    n = pl.program_id(2)
    C = q_ref.shape[0]

    @pl.when(n == 0)
    def _():
        r_ref[...] = jnp.zeros_like(r_ref)

    lg = -jnp.abs(lg_ref[1, h])
    cos, sin = cos_ref[...], sin_ref[...]
    q = _rotary(q_ref[...].astype(F32), cos, sin)
    k = _rotary(k_ref[...].astype(F32), cos, sin) * (RET_DK ** -0.5)
    idx = lax.broadcasted_iota(jnp.int32, (C, 1), 0).astype(F32)
    qd = (q * jnp.exp(lg * (C - 1.0 - idx))).astype(BF16)
    kd = (k * jnp.exp(lg * (idx + 1.0))).astype(BF16)
    r = r_ref[...]
    ob_ref[...] = jnp.dot(qd, r.astype(BF16), preferred_element_type=F32)
    kv = lax.dot_general(kd, v_ref[...], (((0,), (0,)), ((), ())), preferred_element_type=F32)
    r_ref[...] = r * jnp.exp(lg * C) + kv


def _ret_fwd_body(lg_ref, q_ref, k_ref, v_ref, g_ref, cos_ref, sin_ref, ob_ref, gn_ref, o_ref, r_ref, dmat_ref):
    h = pl.program_id(1)
    n = pl.program_id(2)
    C = q_ref.shape[0]
    lgf = -jnp.abs(lg_ref[0, h])
    lgb = -jnp.abs(lg_ref[1, h])

    @pl.when(n == 0)
    def _():
        r_ref[...] = jnp.zeros_like(r_ref)
        ii = lax.broadcasted_iota(jnp.int32, (C, C), 0)
        jj = lax.broadcasted_iota(jnp.int32, (C, C), 1)
        d = (ii - jj).astype(F32)
        dmat_ref[...] = jnp.where(d >= 0, jnp.exp(lgf * jnp.maximum(d, 0.0)), jnp.exp(lgb * jnp.maximum(-d, 0.0)))

    cos, sin = cos_ref[...], sin_ref[...]
    q = _rotary(q_ref[...].astype(F32), cos, sin)
    k = _rotary(k_ref[...].astype(F32), cos, sin) * (RET_DK ** -0.5)
    v = v_ref[...]
    idx = lax.broadcasted_iota(jnp.int32, (C, 1), 0).astype(F32)
    qb = q.astype(BF16)
    kb = k.astype(BF16)
    s = lax.dot_general(qb, kb, (((1,), (1,)), ((), ())), preferred_element_type=F32) * dmat_ref[...]
    r = r_ref[...]
    qd = (q * jnp.exp(lgf * (idx + 1.0))).astype(BF16)
    o = (jnp.dot(s.astype(BF16), v, preferred_element_type=F32)
         + jnp.dot(qd, r.astype(BF16), preferred_element_type=F32)
         + ob_ref[...])
    kd = (k * jnp.exp(lgf * (C - 1.0 - idx))).astype(BF16)
    kv = lax.dot_general(kd, v, (((0,), (0,)), ((), ())), preferred_element_type=F32)
    r_ref[...] = r * jnp.exp(lgf * C) + kv

    mu = jnp.mean(o, axis=-1, keepdims=True)
    oc = o - mu
    var = jnp.mean(oc * oc, axis=-1, keepdims=True)
    y = oc * lax.rsqrt(var + NORM_EPS) * gn_ref[...]
    g = g_ref[...].astype(F32)
    o_ref[...] = (y * (g * jax.nn.sigmoid(g))).astype(o_ref.dtype)


def _retention(proj, B, S, log_decay, gn_w, cos, sin):
    C = min(RET_CHUNK, S)
    N = S // C
    T = B * S
    H = RET_HEADS
    kq = RET_QK // RET_DK
    kv = (2 * RET_QK) // RET_DV
    kg = kv + H
    smem = pl.BlockSpec(memory_space=pltpu.SMEM)
    lg = log_decay.astype(F32)

    def rows_b(b, h, n):
        return b * N + (N - 1 - n)

    ob = pl.pallas_call(
        _ret_bwd_body,
        grid=(B, H, N),
        in_specs=[
            smem,
            pl.BlockSpec((C, RET_DK), lambda b, h, n: (rows_b(b, h, n), h)),
            pl.BlockSpec((C, RET_DK), lambda b, h, n: (rows_b(b, h, n), kq + h)),
            pl.BlockSpec((C, RET_DV), lambda b, h, n: (rows_b(b, h, n), kv + h)),
            pl.BlockSpec((C, RET_DK // 2), lambda b, h, n: (N - 1 - n, 0)),
            pl.BlockSpec((C, RET_DK // 2), lambda b, h, n: (N - 1 - n, 0)),
        ],
        out_specs=pl.BlockSpec((C, RET_DV), lambda b, h, n: (rows_b(b, h, n), h)),
        out_shape=jax.ShapeDtypeStruct((T, RET_V), F32),
        scratch_shapes=[pltpu.VMEM((RET_DK, RET_DV), F32)],
        compiler_params=_cparams("parallel", "parallel", "arbitrary"),
        name="retention_bwd",
    )(lg, proj, proj, proj, cos, sin)

    return pl.pallas_call(
        _ret_fwd_body,
        grid=(B, H, N),
        in_specs=[
            smem,
            pl.BlockSpec((C, RET_DK), lambda b, h, n: (b * N + n, h)),
            pl.BlockSpec((C, RET_DK), lambda b, h, n: (b * N + n, kq + h)),
            pl.BlockSpec((C, RET_DV), lambda b, h, n: (b * N + n, kv + h)),
            pl.BlockSpec((C, RET_DV), lambda b, h, n: (b * N + n, kg + h)),
            pl.BlockSpec((C, RET_DK // 2), lambda b, h, n: (n, 0)),
            pl.BlockSpec((C, RET_DK // 2), lambda b, h, n: (n, 0)),
            pl.BlockSpec((C, RET_DV), lambda b, h, n: (b * N + n, h)),
            pl.BlockSpec((1, RET_DV), lambda b, h, n: (0, h)),
        ],
        out_specs=pl.BlockSpec((C, RET_DV), lambda b, h, n: (b * N + n, h)),
        out_shape=jax.ShapeDtypeStruct((T, RET_V), BF16),
        scratch_shapes=[pltpu.VMEM((RET_DK, RET_DV), F32), pltpu.VMEM((C, C), F32)],
        compiler_params=_cparams("parallel", "parallel", "arbitrary"),
        name="retention_fwd",
    )(lg, proj, proj, proj, proj, cos, sin, ob, gn_w.reshape(1, RET_V).astype(F32))


CONV_HALO = 16


def _conv_body(x_ref, p_ref, nx_ref, w_ref, o_ref, ext_ref, *, normalize, q_blocks):
    i = pl.program_id(1)
    c = pl.program_id(2)
    last = pl.num_programs(1) - 1
    ts, tc = x_ref.shape
    prev = jnp.where(i == 0, 0.0, p_ref[...].astype(F32))
    nxt = jnp.where(i == last, 0.0, nx_ref[...].astype(F32))
    ext_ref[0:8, :] = prev[CONV_HALO - 8:, :]
    ext_ref[8:8 + ts, :] = x_ref[...].astype(F32)
    ext_ref[8 + ts:16 + ts, :] = nxt[:8, :]
    w = w_ref[...]
    half = GDN_CONV // 2
    acc = jnp.zeros((ts, tc), F32)
    for t in range(GDN_CONV):
        acc = acc + ext_ref[pl.ds(8 - half + t, ts), :] * w[t:t + 1, :]
    y = acc * jax.nn.sigmoid(acc)
    if normalize:
        scale = jnp.where(c < q_blocks, GDN_DK ** -0.5, 1.0)
        for hh in range(tc // GDN_DK):
            yh = y[:, hh * GDN_DK:(hh + 1) * GDN_DK]
            ss = jnp.sum(yh * yh, axis=-1, keepdims=True)
            o_ref[:, hh * GDN_DK:(hh + 1) * GDN_DK] = (yh * (lax.rsqrt(ss + NORM_EPS) * scale)).astype(o_ref.dtype)
    else:
        o_ref[...] = y.astype(o_ref.dtype)


def _gdn_conv(proj, B, S, col0, n_ch, conv_w, normalize, ts=512, tc=512):
    ts = min(ts, S)
    T = B * S
    nS = S // ts
    cb0 = col0 // tc
    hr = ts // CONV_HALO
    nH = T // CONV_HALO
    body = functools.partial(_conv_body, normalize=normalize, q_blocks=GDN_QK // tc)
    return pl.pallas_call(
        body,
        grid=(B, nS, n_ch // tc),
        in_specs=[
            pl.BlockSpec((ts, tc), lambda b, i, c: (b * nS + i, cb0 + c)),
            pl.BlockSpec((CONV_HALO, tc), lambda b, i, c: (jnp.maximum((b * nS + i) * hr - 1, 0), cb0 + c)),
            pl.BlockSpec((CONV_HALO, tc), lambda b, i, c: (jnp.minimum((b * nS + i + 1) * hr, nH - 1), cb0 + c)),
            pl.BlockSpec((GDN_CONV, tc), lambda b, i, c: (0, c)),
        ],
        out_specs=pl.BlockSpec((ts, tc), lambda b, i, c: (b * nS + i, c)),
        out_shape=jax.ShapeDtypeStruct((T, n_ch), BF16),
        scratch_shapes=[pltpu.VMEM((ts + 16, tc), F32)],
        compiler_params=_cparams("parallel", "parallel", "parallel"),
        name="gdn_conv",
    )(proj, proj, proj, conv_w.astype(F32))


def _softplus(x):
    return jnp.maximum(x, 0.0) + jnp.log1p(jnp.exp(-jnp.abs(x)))


def _gdn_body(q_ref, k_ref, v_ref, col_ref, row_ref, pcol_ref, prow_ref, *rest, reverse, final):
    if final:
        ob_ref, z_ref, nw_ref, o_ref, s_ref = rest
    else:
        o_ref, s_ref = rest
    n = pl.program_id(2)
    C = q_ref.shape[0]
    HB = GDN_HB

    @pl.when(n == 0)
    def _():
        s_ref[...] = jnp.zeros_like(s_ref)

    ii = lax.broadcasted_iota(jnp.int32, (C, C), 0)
    jj = lax.broadcasted_iota(jnp.int32, (C, C), 1)
    if reverse:
        tri, strict = ii <= jj, ii < jj
    else:
        tri, strict = ii >= jj, ii > jj
    tri_f = tri.astype(F32)
    eye = (ii == jj).astype(F32)
    end = 0 if reverse else C - 1

    col = col_ref[0]
    row = row_ref[0, 0]
    pcol = pcol_ref[0]
    prow = prow_ref[0]
    beta_col = jax.nn.sigmoid(col)
    g_col = -jnp.exp(pcol[0:1, :]) * _softplus(col + pcol[1:2, :])
    g_row = -jnp.exp(prow[:, 0:1]) * _softplus(row + prow[:, 1:2])
    hi = lax.Precision.HIGHEST
    dec_col = jnp.dot(tri_f, g_col, precision=hi, preferred_element_type=F32)
    dec_row = lax.dot_general(g_row, tri_f, (((1,), (1,)), ((), ())), precision=hi,
                              preferred_element_type=F32)

    for pair in range(HB // 2):
        qh = q_ref[:, pair * GDN_DK:(pair + 1) * GDN_DK]
        kh = k_ref[:, pair * GDN_DK:(pair + 1) * GDN_DK]
        qk = jnp.concatenate([qh, kh], axis=0)
        qkk = lax.dot_general(qk, kh, (((1,), (1,)), ((), ())), preferred_element_type=F32)
        qkt, kkt = qkk[:C], qkk[C:]
        kf = kh.astype(F32)
        qf = qh.astype(F32)
        for sub in range(2):
            hh = 2 * pair + sub
            bcol = beta_col[:, hh:hh + 1]
            dcol = dec_col[:, HB + hh:HB + hh + 1]
            drow = dec_row[HB + hh:HB + hh + 1, :]
            dend = dcol[end:end + 1, :]
            L = jnp.where(tri, jnp.exp(jnp.where(tri, dcol - drow, 0.0)), 0.0)
            ecol = jnp.exp(dcol)
            A = jnp.where(strict, kkt * L * bcol, 0.0)
            P = -A
            Tm = eye + P
            steps = max(1, int(math.ceil(math.log2(C))) - 1)
            for _ in range(steps):
                Pb = P.astype(BF16)
                P = jnp.dot(Pb, Pb, preferred_element_type=F32)
                Tm = Tm + jnp.dot(Tm.astype(BF16), P.astype(BF16), preferred_element_type=F32)
            vh = v_ref[:, hh * GDN_DV:(hh + 1) * GDN_DV].astype(F32)
            rhs = jnp.concatenate([vh * bcol, kf * (bcol * ecol)], axis=-1).astype(BF16)
            sol = jnp.dot(Tm.astype(BF16), rhs, preferred_element_type=F32)
            u, w = sol[:, :GDN_DV], sol[:, GDN_DV:]
            St = s_ref[hh]
            Sb = St.astype(BF16)
            v_new = u - jnp.dot(w.astype(BF16), Sb, preferred_element_type=F32)
            vnb = v_new.astype(BF16)
            attn = (qkt * L).astype(BF16)
            o = (jnp.dot((qf * ecol).astype(BF16), Sb, preferred_element_type=F32)
                 + jnp.dot(attn, vnb, preferred_element_type=F32))
            k_tail = (kf * jnp.exp(dend - dcol)).astype(BF16)
            s_ref[hh] = St * jnp.exp(dend) + lax.dot_general(
                k_tail, vnb, (((0,), (0,)), ((), ())), preferred_element_type=F32)
            sl = slice(hh * GDN_DV, (hh + 1) * GDN_DV)
            if final:
                o = o + ob_ref[:, sl]
                y = o * lax.rsqrt(jnp.mean(o * o, axis=-1, keepdims=True) + NORM_EPS) * nw_ref[...]
                z = z_ref[:, sl].astype(F32)
                o_ref[:, sl] = (y * (z * jax.nn.sigmoid(z))).astype(o_ref.dtype)
            else:
                o_ref[:, sl] = o


def _gdn_side_inputs(ba, a_log, dt_bias, B, S, C, d):
    T = B * S
    N = S // C
    Hv, HB = GDN_V_HEADS, GDN_HB
    nhb = Hv // HB
    bl = ba[:, d * Hv:(d + 1) * Hv].reshape(T, nhb, HB)
    al = ba[:, 2 * Hv + d * Hv:2 * Hv + (d + 1) * Hv].reshape(T, nhb, HB)
    col = jnp.concatenate([bl, al], axis=-1).transpose(1, 0, 2)
    row = col.reshape(nhb, B * N, C, 2 * HB).transpose(0, 1, 3, 2)
    zeros = jnp.zeros((nhb, HB), F32)
    p0 = jnp.concatenate([zeros, a_log[d].astype(F32).reshape(nhb, HB)], axis=-1)
    p1 = jnp.concatenate([zeros, dt_bias[d].astype(F32).reshape(nhb, HB)], axis=-1)
    pcol = jnp.stack([p0, p1], axis=1)
    prow = jnp.stack([p0, p1], axis=2)
    return col, row, pcol, prow


def _gdn_direction(qk_c, v_c, ba, a_log, dt_bias, B, S, d, final_inputs=None):
    C = min(GDN_CHUNK, S)
    N = S // C
    T = B * S
    HB = GDN_HB
    nhb = GDN_V_HEADS // HB
    reverse = d == 1
    final = final_inputs is not None
    col, row, pcol, prow = _gdn_side_inputs(ba, a_log, dt_bias, B, S, C, d)
    qw = (HB // 2) * GDN_DK
    vw = HB * GDN_DV
    k_off = GDN_QK // qw

    def ch(n):
        return (N - 1 - n) if reverse else n

    in_specs = [
        pl.BlockSpec((C, qw), lambda b, hb, n: (b * N + ch(n), hb)),
        pl.BlockSpec((C, qw), lambda b, hb, n: (b * N + ch(n), k_off + hb)),
        pl.BlockSpec((C, vw), lambda b, hb, n: (b * N + ch(n), hb)),
        pl.BlockSpec((1, C, 2 * HB), lambda b, hb, n: (hb, b * N + ch(n), 0)),
        pl.BlockSpec((1, 1, 2 * HB, C), lambda b, hb, n: (hb, b * N + ch(n), 0, 0)),
        pl.BlockSpec((1, 2, 2 * HB), lambda b, hb, n: (hb, 0, 0)),
        pl.BlockSpec((1, 2 * HB, 2), lambda b, hb, n: (hb, 0, 0)),
    ]
    args = [qk_c, qk_c, v_c, col, row, pcol, prow]
    if final:
        o_other, proj, z_col0, norm_w = final_inputs
        zb = z_col0 // vw
        in_specs += [
            pl.BlockSpec((C, vw), lambda b, hb, n: (b * N + ch(n), hb)),
            pl.BlockSpec((C, vw), lambda b, hb, n: (b * N + ch(n), zb + hb)),
            pl.BlockSpec((1, GDN_DV), lambda b, hb, n: (0, 0)),
        ]
        args += [o_other, proj, norm_w.reshape(1, GDN_DV).astype(F32)]
    return pl.pallas_call(
        functools.partial(_gdn_body, reverse=reverse, final=final),
        grid=(B, nhb, N),
        in_specs=in_specs,
        out_specs=pl.BlockSpec((C, vw), lambda b, hb, n: (b * N + ch(n), hb)),
        out_shape=jax.ShapeDtypeStruct((T, GDN_V), BF16 if final else F32),
        scratch_shapes=[pltpu.VMEM((HB, GDN_DK, GDN_DV), F32)],
        compiler_params=_cparams("parallel", "parallel", "arbitrary"),
        name="gdn_final" if final else "gdn_first",
    )(*args)


def _merge_body(a0_ref, a1_ref, w0_ref, w1_ref, g0_ref, g1_ref, o_ref):
    p0 = jnp.dot(a0_ref[...], w0_ref[0], preferred_element_type=F32)
    p1 = jnp.dot(a1_ref[...], w1_ref[0], preferred_element_type=F32)
    g0 = jax.nn.sigmoid(g0_ref[...].astype(F32))
    g1 = jax.nn.sigmoid(g1_ref[...].astype(F32))
    o_ref[...] = (g0 * p0 + g1 * p1).astype(o_ref.dtype)


def _merge(ret, gdn, w_branch, gate_logits, tm=512, tn=512):
    T, K = ret.shape
    D = w_branch.shape[-1]
    tm = min(tm, T)
    nj = D // tn
    return pl.pallas_call(
        _merge_body,
        grid=(T // tm, nj),
        in_specs=[
            pl.BlockSpec((tm, K), lambda i, j: (i, 0)),
            pl.BlockSpec((tm, K), lambda i, j: (i, 0)),
            pl.BlockSpec((1, K, tn), lambda i, j: (0, 0, j)),
            pl.BlockSpec((1, K, tn), lambda i, j: (1, 0, j)),
            pl.BlockSpec((tm, tn), lambda i, j: (i, j)),
            pl.BlockSpec((tm, tn), lambda i, j: (i, nj + j)),
        ],
        out_specs=pl.BlockSpec((tm, tn), lambda i, j: (i, j)),
        out_shape=jax.ShapeDtypeStruct((T, D), BF16),
        compiler_params=_cparams("parallel", "arbitrary"),
        name="branch_merge",
    )(ret, gdn, w_branch, w_branch, gate_logits, gate_logits)


def _xattn_body(q_ref, k_ref, v_ref, o_ref):
    scale = XA_DH ** -0.5
    for h in range(XA_HEADS):
        sl = slice(h * XA_DH, (h + 1) * XA_DH)
        s = lax.dot_general(q_ref[:, sl], k_ref[:, sl], (((1,), (1,)), ((), ())),
                            preferred_element_type=F32) * scale
        m = jnp.max(s, axis=-1, keepdims=True)
        e = jnp.exp(s - m)
        p = e / jnp.sum(e, axis=-1, keepdims=True)
        o_ref[:, sl] = jnp.dot(p.astype(BF16), v_ref[:, sl], preferred_element_type=F32).astype(o_ref.dtype)


def _xattn(q, kv, B, S, M, tq=512):
    T, D = q.shape
    tq = min(tq, S)
    nq = S // tq
    return pl.pallas_call(
        _xattn_body,
        grid=(B, nq),
        in_specs=[
            pl.BlockSpec((tq, D), lambda b, i: (b * nq + i, 0)),
            pl.BlockSpec((M, D), lambda b, i: (b, 0)),
            pl.BlockSpec((M, D), lambda b, i: (b, 1)),
        ],
        out_specs=pl.BlockSpec((tq, D), lambda b, i: (b * nq + i, 0)),
        out_shape=jax.ShapeDtypeStruct((T, D), BF16),
        compiler_params=_cparams("parallel", "arbitrary"),
        name="cross_attention",
    )(q, kv, kv)


META_IDX, META_GATE, META_RANK = 0, TOP_K, 2 * TOP_K


def _router_body(x_ref, g_ref, w_ref, b_ref, h_ref, meta_ref, cnt_ref, carry_ref):
    i = pl.program_id(0)
    tm = x_ref.shape[0]

    @pl.when(i == 0)
    def _():
        carry_ref[...] = jnp.zeros_like(carry_ref)

    x = x_ref[...]
    h = x * lax.rsqrt(jnp.mean(x * x, axis=-1, keepdims=True) + NORM_EPS) * g_ref[...]
    h_ref[...] = h
    logits = jnp.dot(h, w_ref[...], precision=lax.Precision.HIGHEST, preferred_element_type=F32) + b_ref[...]
    lane = lax.broadcasted_iota(jnp.int32, (tm, LANES), 1)
    neg = jnp.float32(-jnp.inf)
    logits = jnp.where(lane < N_EXPERTS, logits, neg)
    vals, hots = [], []
    for _ in range(TOP_K):
        m = jnp.max(logits, axis=-1, keepdims=True)
        first = jnp.min(jnp.where(logits == m, lane, LANES), axis=-1, keepdims=True)
        hot = lane == first
        vals.append(m)
        hots.append(hot)
        logits = jnp.where(hot, neg, logits)
    es = [jnp.exp(v - vals[0]) for v in vals]
    den = es[0]
    for e in es[1:]:
        den = den + e
    mh = jnp.zeros((tm, LANES), F32)
    for hot in hots:
        mh = mh + hot.astype(F32)
    ii = lax.broadcasted_iota(jnp.int32, (tm, tm), 0)
    jj = lax.broadcasted_iota(jnp.int32, (tm, tm), 1)
    before = (ii > jj).astype(BF16)
    ranks = jnp.dot(before, mh.astype(BF16), preferred_element_type=F32) + carry_ref[...]
    lane_f = lane.astype(F32)
    meta = jnp.zeros((tm, LANES), F32)
    for kk in range(TOP_K):
        e_k = jnp.sum(jnp.where(hots[kk], lane_f, 0.0), axis=-1, keepdims=True)
        r_k = jnp.sum(jnp.where(hots[kk], ranks, 0.0), axis=-1, keepdims=True)
        meta = meta + jnp.where(lane == META_IDX + kk, e_k, 0.0)
        meta = meta + jnp.where(lane == META_GATE + kk, es[kk] / den, 0.0)
        meta = meta + jnp.where(lane == META_RANK + kk, r_k, 0.0)
    meta_ref[...] = meta
    carry_ref[...] = carry_ref[...] + jnp.sum(mh, axis=0, keepdims=True)
    cnt_ref[...] = carry_ref[...]


def _router(x, gain, w_router, b_router, tm=512):
    T, D = x.shape
    tm = min(tm, T)
    wr = jnp.zeros((D, LANES), F32).at[:, :N_EXPERTS].set(w_router.astype(F32))
    br = jnp.zeros((1, LANES), F32).at[0, :N_EXPERTS].set(b_router.astype(F32))
    return pl.pallas_call(
        _router_body,
        grid=(T // tm,),
        in_specs=[
            pl.BlockSpec((tm, D), lambda i: (i, 0)),
            pl.BlockSpec((1, D), lambda i: (0, 0)),
            pl.BlockSpec((D, LANES), lambda i: (0, 0)),
            pl.BlockSpec((1, LANES), lambda i: (0, 0)),
        ],
        out_specs=[
            pl.BlockSpec((tm, D), lambda i: (i, 0)),
            pl.BlockSpec((tm, LANES), lambda i: (i, 0)),
            pl.BlockSpec((1, LANES), lambda i: (0, 0)),
        ],
        out_shape=[
            jax.ShapeDtypeStruct((T, D), F32),
            jax.ShapeDtypeStruct((T, LANES), F32),
            jax.ShapeDtypeStruct((1, LANES), F32),
        ],
        scratch_shapes=[pltpu.VMEM((1, LANES), F32)],
        compiler_params=_cparams("arbitrary"),
        name="moe_router",
    )(x, gain.reshape(1, D).astype(F32), wr, br)


def _dispatch_body(dest_ref, h_ref, init_ref, xs_ref, sem):
    del init_ref
    tm = h_ref.shape[0]

    def row_copy(r, kk):
        return pltpu.make_async_copy(h_ref.at[pl.ds(r, 1)], xs_ref.at[pl.ds(dest_ref[r * TOP_K + kk], 1)], sem)

    def issue(r, carry):
        for kk in range(TOP_K):
            row_copy(r, kk).start()
        return carry

    def drain(r, carry):
        for kk in range(TOP_K):
            row_copy(r, kk).wait()
        return carry

    lax.fori_loop(0, tm, issue, 0)
    lax.fori_loop(0, tm, drain, 0)


def _dispatch(h, dest_flat, n_slots, tm=256):
    T, D = h.shape
    tm = min(tm, T)
    init = jnp.zeros((n_slots, D), F32)
    return pl.pallas_call(
        _dispatch_body,
        grid=(T // tm,),
        in_specs=[
            pl.BlockSpec((tm * TOP_K,), lambda i: (i,), memory_space=pltpu.SMEM),
            pl.BlockSpec((tm, D), lambda i: (i, 0)),
            pl.BlockSpec(memory_space=pl.ANY),
        ],
        out_specs=pl.BlockSpec(memory_space=pl.ANY),
        out_shape=jax.ShapeDtypeStruct((n_slots, D), F32),
        scratch_shapes=[pltpu.SemaphoreType.DMA(())],
        input_output_aliases={2: 0},
        compiler_params=_cparams("arbitrary"),
        name="moe_dispatch",
    )(dest_flat, h, init)


def _expert_body(be_ref, nb_ref, x_ref, wg_ref, wu_ref, bg_ref, bu_ref, wd_ref, bd_ref, o_ref, xb_ref, acc_ref):
    blk = pl.program_id(0)
    j = pl.program_id(1)
    nj = pl.num_programs(1)
    used = blk < nb_ref[0]

    @pl.when(used)
    def _():
        @pl.when(j == 0)
        def _():
            xb_ref[...] = x_ref[...].astype(BF16)

        xb = xb_ref[...]
        g = jnp.dot(xb, wg_ref[0], preferred_element_type=F32) + bg_ref[0]
        u = jnp.dot(xb, wu_ref[0], preferred_element_type=F32) + bu_ref[0]
        g = jnp.minimum(g, SWIGLU_LIMIT)
        u = jnp.clip(u, -SWIGLU_LIMIT, SWIGLU_LIMIT)
        act = g * jax.nn.sigmoid(SWIGLU_ALPHA * g) * (u + 1.0)
        part = jnp.dot(act.astype(BF16), wd_ref[0], preferred_element_type=F32)

        @pl.when(j == 0)
        def _():
            acc_ref[...] = part

        @pl.when(j > 0)
        def _():
            acc_ref[...] = acc_ref[...] + part

        @pl.when(j == nj - 1)
        def _():
            o_ref[...] = acc_ref[...] + bd_ref[0]

    @pl.when(jnp.logical_and(jnp.logical_not(used), j == nj - 1))
    def _():
        o_ref[...] = jnp.zeros_like(o_ref)


def _experts(xs, block_e, nb_used, wg, wu, bg, bu, wd, bd, tf=512):
    n_slots, D = xs.shape
    NB = n_slots // MOE_BLOCK
    nj = D_FF // tf

    def xrow(blk, j, be, nb):
        return (jnp.minimum(blk, nb[0] - 1), 0)

    grid_spec = pltpu.PrefetchScalarGridSpec(
        num_scalar_prefetch=2,
        grid=(NB, nj),
        in_specs=[
            pl.BlockSpec((MOE_BLOCK, D), xrow),
            pl.BlockSpec((1, D, tf), lambda blk, j, be, nb: (be[blk], 0, j)),
            pl.BlockSpec((1, D, tf), lambda blk, j, be, nb: (be[blk], 0, j)),
            pl.BlockSpec((1, 1, tf), lambda blk, j, be, nb: (be[blk], 0, j)),
            pl.BlockSpec((1, 1, tf), lambda blk, j, be, nb: (be[blk], 0, j)),
            pl.BlockSpec((1, tf, D), lambda blk, j, be, nb: (be[blk], j, 0)),
            pl.BlockSpec((1, 1, D), lambda blk, j, be, nb: (be[blk], 0, 0)),
        ],
        out_specs=pl.BlockSpec((MOE_BLOCK, D), lambda blk, j, be, nb: (blk, 0)),
        scratch_shapes=[pltpu.VMEM((MOE_BLOCK, D), BF16), pltpu.VMEM((MOE_BLOCK, D), F32)],
    )
    return pl.pallas_call(
        _expert_body,
        grid_spec=grid_spec,
        out_shape=jax.ShapeDtypeStruct((n_slots, D), F32),
        compiler_params=_cparams("arbitrary", "arbitrary"),
        name="moe_experts",
    )(block_e, nb_used, xs, wg, wu, bg, bu, wd, bd)


def _combine_body(dest_ref, ys_ref, meta_ref, x_ref, g_ref, o_ref, buf_ref, sem):
    tm = x_ref.shape[0]

    def row_copy(r, kk):
        return pltpu.make_async_copy(ys_ref.at[pl.ds(dest_ref[r * TOP_K + kk], 1)],
                                     buf_ref.at[kk, pl.ds(r, 1)], sem)

    def issue(r, carry):
        for kk in range(TOP_K):
            row_copy(r, kk).start()
        return carry

    def drain(r, carry):
        for kk in range(TOP_K):
            row_copy(r, kk).wait()
        return carry

    lax.fori_loop(0, tm, issue, 0)
    lax.fori_loop(0, tm, drain, 0)
    meta = meta_ref[...]
    y = x_ref[...]
    for kk in range(TOP_K):
        y = y + buf_ref[kk] * meta[:, META_GATE + kk:META_GATE + kk + 1]
    o_ref[...] = y * lax.rsqrt(jnp.mean(y * y, axis=-1, keepdims=True) + NORM_EPS) * g_ref[...]


def _combine(ys, dest_flat, meta, x, gain, tm=256):
    T, D = x.shape
    tm = min(tm, T)
    return pl.pallas_call(
        _combine_body,
        grid=(T // tm,),
        in_specs=[
            pl.BlockSpec((tm * TOP_K,), lambda i: (i,), memory_space=pltpu.SMEM),
            pl.BlockSpec(memory_space=pl.ANY),
            pl.BlockSpec((tm, LANES), lambda i: (i, 0)),
            pl.BlockSpec((tm, D), lambda i: (i, 0)),
            pl.BlockSpec((1, D), lambda i: (0, 0)),
        ],
        out_specs=pl.BlockSpec((tm, D), lambda i: (i, 0)),
        out_shape=jax.ShapeDtypeStruct((T, D), F32),
        scratch_shapes=[pltpu.VMEM((TOP_K, tm, D), F32), pltpu.SemaphoreType.DMA(())],
        compiler_params=_cparams("arbitrary"),
        name="moe_combine",
    )(dest_flat, ys, meta, x, gain.reshape(1, D).astype(F32))


def _moe_final(x2, ln_moe, w_router, b_router, ew, ln_final):
    T, D = x2.shape
    A = T * TOP_K
    NB = A // MOE_BLOCK + N_EXPERTS
    h, meta, cnt = _router(x2, ln_moe, w_router, b_router)
    idx = meta[:, META_IDX:META_IDX + TOP_K].astype(jnp.int32)
    rank = meta[:, META_RANK:META_RANK + TOP_K].astype(jnp.int32)
    counts = cnt[0, :N_EXPERTS].astype(jnp.int32)
    padded = (counts + MOE_BLOCK - 1) // MOE_BLOCK * MOE_BLOCK
    pad_end = jnp.cumsum(padded)
    pad_start = pad_end - padded
    dest = (pad_start[idx] + rank).reshape(A)
    block_e = jnp.minimum(jnp.searchsorted(pad_end, jnp.arange(NB, dtype=jnp.int32) * MOE_BLOCK, side='right'),
                          N_EXPERTS - 1).astype(jnp.int32)
    nb_used = (pad_end[-1:] // MOE_BLOCK).astype(jnp.int32)
    xs = _dispatch(h, dest, NB * MOE_BLOCK)
    ys = _experts(xs, block_e, nb_used, *ew)
    return _combine(ys, dest, meta, x2, ln_final)


def _prepare_weights(w_in, w_branch, w_out, xa_w_q, xa_w_kv, xa_w_o, w_gate_up, b_gate_up, w_down, b_down):
    main_w = 2 * RET_QK + 2 * RET_V + 2 * GDN_QK + 2 * GDN_V
    ba_w = 4 * GDN_V_HEADS
    w = {}
    w['in_main'] = w_in[:, :main_w].astype(BF16)
    w['in_ba'] = w_in[:, main_w:main_w + ba_w].astype(BF16)
    w['in_gate'] = w_in[:, main_w + ba_w:].astype(BF16)
    w['branch'] = w_branch.astype(BF16)
    w['out'] = w_out.astype(BF16)
    w['xa_q'] = xa_w_q.astype(BF16)
    w['xa_kv'] = xa_w_kv.astype(BF16)
    w['xa_o'] = xa_w_o.astype(BF16)
    E = w_gate_up.shape[0]
    gu = w_gate_up.reshape(E, D_MODEL, D_FF, 2)
    bgu = b_gate_up.reshape(E, 1, D_FF, 2).astype(F32)
    w['experts'] = (gu[..., 0].astype(BF16), gu[..., 1].astype(BF16), bgu[..., 0], bgu[..., 1],
                    w_down.astype(BF16), b_down.reshape(E, 1, D_MODEL).astype(F32))
    return w


def _rope_tables(S):
    inv_freq = ROPE_BASE ** (-jnp.arange(0, RET_DK, 2, dtype=F32) / RET_DK)
    ang = jnp.arange(S, dtype=F32)[:, None] * inv_freq[None, :]
    return jnp.cos(ang), jnp.sin(ang)


def _encoder(x, mem, w, p):
    B, S, D = x.shape
    M = mem.shape[1]
    T = B * S
    x0 = x.reshape(T, D)
    h = _rmsnorm(x0, p['ln_mix'], BF16)
    proj = _matmul(h, w['in_main'], BF16, name="in_proj")
    ba = _matmul(h, w['in_ba'], F32, name="in_proj_ba")
    gate_logits = _matmul(h, w['in_gate'], BF16, name="in_proj_gate")
    cos, sin = _rope_tables(S)
    ret = _retention(proj, B, S, p['ret_log_decay'], p['ret_gn_w'], cos, sin)
    g0 = 2 * RET_QK + 2 * RET_V
    qk_c = _gdn_conv(proj, B, S, g0, 2 * GDN_QK, p['gdn_conv_w'][:, :2 * GDN_QK], True)
    v_c = _gdn_conv(proj, B, S, g0 + 2 * GDN_QK, GDN_V, p['gdn_conv_w'][:, 2 * GDN_QK:], False)
    o_b = _gdn_direction(qk_c, v_c, ba, p['gdn_a_log'], p['gdn_dt_bias'], B, S, 1)
    gdn = _gdn_direction(qk_c, v_c, ba, p['gdn_a_log'], p['gdn_dt_bias'], B, S, 0,
                         final_inputs=(o_b, proj, g0 + 2 * GDN_QK + GDN_V, p['gdn_norm_w']))
    merged = _merge(ret, gdn, w['branch'], gate_logits)
    x1 = _matmul(merged, w['out'], F32, residual=x0, name="mixer_out")
    hq = _rmsnorm(x1, p['ln_xa'], BF16)
    hm = _rmsnorm(mem.reshape(B * M, D), p['ln_mem'], BF16)
    q = _matmul(hq, w['xa_q'], BF16, name="xa_q")
    kv = _matmul(hm, w['xa_kv'], BF16, name="xa_kv")
    att = _xattn(q, kv, B, S, M)
    x2 = _matmul(att, w['xa_o'], F32, residual=x1, name="xa_out")
    y = _moe_final(x2, p['ln_moe'], p['w_router'], p['b_router'], w['experts'], p['ln_final'])
    return y.reshape(B, S, D)


def kernel(x_prompt, x_sample, mem_prompt, mem_sample, ln_mix, w_in, ret_log_decay, ret_gn_w, gdn_conv_w, gdn_a_log, gdn_dt_bias, gdn_norm_w, w_branch, w_out, ln_xa, ln_mem, xa_w_q, xa_w_kv, xa_w_o, ln_moe, w_router, b_router, w_gate_up, b_gate_up, w_down, b_down, ln_final):
    depth = w_in.shape[0]
    assert depth == 1, "single-layer trunk"
    l = 0
    w = _prepare_weights(w_in[l], w_branch[l], w_out[l], xa_w_q[l], xa_w_kv[l], xa_w_o[l],
                         w_gate_up[l], b_gate_up[l], w_down[l], b_down[l])
    p = dict(ln_mix=ln_mix[l], ret_log_decay=ret_log_decay[l], ret_gn_w=ret_gn_w[l], gdn_conv_w=gdn_conv_w[l],
             gdn_a_log=gdn_a_log[l], gdn_dt_bias=gdn_dt_bias[l], gdn_norm_w=gdn_norm_w[l], ln_xa=ln_xa[l],
             ln_mem=ln_mem[l], ln_moe=ln_moe[l], w_router=w_router[l], b_router=b_router[l], ln_final=ln_final)
    y_prompt = _encoder(x_prompt, mem_prompt, w, p)
    y_sample = _encoder(x_sample, mem_sample, w, p)
    return (y_prompt, y_sample)
```

```python
import functools
import math

import jax
import jax.numpy as jnp
from jax import lax
from jax.experimental import pallas as pl
from jax.experimental.pallas import tpu as pltpu

F32 = jnp.float32
BF16 = jnp.bfloat16

D_MODEL = 2048
RET_HEADS = 8
RET_DK = 256
RET_DV = 512
ROPE_BASE = 10000.0
GDN_QK_HEADS = 16
GDN_V_HEADS = 32
GDN_DK = 128
GDN_DV = 128
GDN_CONV = 5
XA_HEADS = 4
XA_DH = D_MODEL // XA_HEADS
N_EXPERTS = 32
TOP_K = 4
D_FF = 2048
SWIGLU_LIMIT = 7.0
SWIGLU_ALPHA = 1.702
MOE_BLOCK = 512
NORM_EPS = 1e-6
N_BRANCH = 2
RET_QK = RET_HEADS * RET_DK
RET_V = RET_HEADS * RET_DV
GDN_QK = GDN_QK_HEADS * GDN_DK
GDN_V = GDN_V_HEADS * GDN_DV

LANES = 128
VMEM_LIMIT = 56 * 1024 * 1024

RET_CHUNK = 256
GDN_CHUNK = 64
GDN_HB = 8


def _cparams(*sem):
    return pltpu.CompilerParams(dimension_semantics=sem, vmem_limit_bytes=VMEM_LIMIT)


def _rmsnorm_body(x_ref, g_ref, o_ref):
    x = x_ref[...].astype(F32)
    ms = jnp.mean(x * x, axis=-1, keepdims=True)
    o_ref[...] = (x * lax.rsqrt(ms + NORM_EPS) * g_ref[...]).astype(o_ref.dtype)


def _rmsnorm(x, gain, out_dtype, tm=512):
    T, D = x.shape
    tm = min(tm, T)
    return pl.pallas_call(
        _rmsnorm_body,
        grid=(T // tm,),
        in_specs=[pl.BlockSpec((tm, D), lambda i: (i, 0)), pl.BlockSpec((1, D), lambda i: (0, 0))],
        out_specs=pl.BlockSpec((tm, D), lambda i: (i, 0)),
        out_shape=jax.ShapeDtypeStruct((T, D), out_dtype),
        compiler_params=_cparams("parallel"),
        name="rmsnorm",
    )(x, gain.reshape(1, D).astype(F32))


def _mm_body(a_ref, b_ref, o_ref):
    o_ref[...] = jnp.dot(a_ref[...], b_ref[...], preferred_element_type=F32).astype(o_ref.dtype)


def _mm_res_body(a_ref, b_ref, r_ref, o_ref):
    o_ref[...] = r_ref[...] + jnp.dot(a_ref[...], b_ref[...], preferred_element_type=F32)


def _matmul(a, b, out_dtype, residual=None, tm=512, tn=1024, name="matmul"):
    M, K = a.shape
    N = b.shape[1]
    tm = min(tm, M)
    tn = min(tn, N)
    in_specs = [pl.BlockSpec((tm, K), lambda i, j: (i, 0)), pl.BlockSpec((K, tn), lambda i, j: (0, j))]
    args = [a, b]
    body = _mm_body
    if residual is not None:
        in_specs.append(pl.BlockSpec((tm, tn), lambda i, j: (i, j)))
        args.append(residual)
        body = _mm_res_body
    return pl.pallas_call(
        body,
        grid=(M // tm, N // tn),
        in_specs=in_specs,
        out_specs=pl.BlockSpec((tm, tn), lambda i, j: (i, j)),
        out_shape=jax.ShapeDtypeStruct((M, N), out_dtype),
        compiler_params=_cparams("parallel", "arbitrary"),
        name=name,
    )(*args)


def _rotary(x, cos, sin):
    half = x.shape[-1] // 2
    x1, x2 = x[:, :half], x[:, half:]
    return jnp.concatenate([x1 * cos - x2 * sin, x1 * sin + x2 * cos], axis=-1)


def _ret_bwd_body(lg_ref, q_ref, k_ref, v_ref, cos_ref, sin_ref, ob_ref, r_ref):
    h = pl.program_id(1)
    n = pl.program_id(2)
    C = q_ref.shape[0]

    @pl.when(n == 0)
    def _():
        r_ref[...] = jnp.zeros_like(r_ref)

    lg = -jnp.abs(lg_ref[1, h])
    cos, sin = cos_ref[...], sin_ref[...]
    q = _rotary(q_ref[...].astype(F32), cos, sin)
    k = _rotary(k_ref[...].astype(F32), cos, sin) * (RET_DK ** -0.5)
    idx = lax.broadcasted_iota(jnp.int32, (C, 1), 0).astype(F32)
    qd = (q * jnp.exp(lg * (C - 1.0 - idx))).astype(BF16)
    kd = (k * jnp.exp(lg * (idx + 1.0))).astype(BF16)
    r = r_ref[...]
    ob_ref[...] = jnp.dot(qd, r.astype(BF16), preferred_element_type=F32)
    kv = lax.dot_general(kd, v_ref[...], (((0,), (0,)), ((), ())), preferred_element_type=F32)
    r_ref[...] = r * jnp.exp(lg * C) + kv


def _ret_fwd_body(lg_ref, q_ref, k_ref, v_ref, g_ref, cos_ref, sin_ref, ob_ref, gn_ref, o_ref, r_ref, dmat_ref):
    h = pl.program_id(1)
    n = pl.program_id(2)
    C = q_ref.shape[0]
    lgf = -jnp.abs(lg_ref[0, h])
    lgb = -jnp.abs(lg_ref[1, h])

    @pl.when(n == 0)
    def _():
        r_ref[...] = jnp.zeros_like(r_ref)
        ii = lax.broadcasted_iota(jnp.int32, (C, C), 0)
        jj = lax.broadcasted_iota(jnp.int32, (C, C), 1)
        d = (ii - jj).astype(F32)
        dmat_ref[...] = jnp.where(d >= 0, jnp.exp(lgf * jnp.maximum(d, 0.0)), jnp.exp(lgb * jnp.maximum(-d, 0.0)))

    cos, sin = cos_ref[...], sin_ref[...]
    q = _rotary(q_ref[...].astype(F32), cos, sin)
    k = _rotary(k_ref[...].astype(F32), cos, sin) * (RET_DK ** -0.5)
    v = v_ref[...]
    idx = lax.broadcasted_iota(jnp.int32, (C, 1), 0).astype(F32)
    qb = q.astype(BF16)
    kb = k.astype(BF16)
    s = lax.dot_general(qb, kb, (((1,), (1,)), ((), ())), preferred_element_type=F32) * dmat_ref[...]
    r = r_ref[...]
    qd = (q * jnp.exp(lgf * (idx + 1.0))).astype(BF16)
    o = (jnp.dot(s.astype(BF16), v, preferred_element_type=F32)
         + jnp.dot(qd, r.astype(BF16), preferred_element_type=F32)
         + ob_ref[...])
    kd = (k * jnp.exp(lgf * (C - 1.0 - idx))).astype(BF16)
    kv = lax.dot_general(kd, v, (((0,), (0,)), ((), ())), preferred_element_type=F32)
    r_ref[...] = r * jnp.exp(lgf * C) + kv

    mu = jnp.mean(o, axis=-1, keepdims=True)
    oc = o - mu
    var = jnp.mean(oc * oc, axis=-1, keepdims=True)
    y = oc * lax.rsqrt(var + NORM_EPS) * gn_ref[...]
    g = g_ref[...].astype(F32)
    o_ref[...] = (y * (g * jax.nn.sigmoid(g))).astype(o_ref.dtype)


def _retention(proj, B, S, log_decay, gn_w, cos, sin):
    C = min(RET_CHUNK, S)
    N = S // C
    T = B * S
    H = RET_HEADS
    kq = RET_QK // RET_DK
    kv = (2 * RET_QK) // RET_DV
    kg = kv + H
    smem = pl.BlockSpec(memory_space=pltpu.SMEM)
    lg = log_decay.astype(F32)

    def rows_b(b, h, n):
        return b * N + (N - 1 - n)

    ob = pl.pallas_call(
        _ret_bwd_body,
        grid=(B, H, N),
        in_specs=[
            smem,
            pl.BlockSpec((C, RET_DK), lambda b, h, n: (rows_b(b, h, n), h)),
            pl.BlockSpec((C, RET_DK), lambda b, h, n: (rows_b(b, h, n), kq + h)),
            pl.BlockSpec((C, RET_DV), lambda b, h, n: (rows_b(b, h, n), kv + h)),
            pl.BlockSpec((C, RET_DK // 2), lambda b, h, n: (N - 1 - n, 0)),
            pl.BlockSpec((C, RET_DK // 2), lambda b, h, n: (N - 1 - n, 0)),
        ],
        out_specs=pl.BlockSpec((C, RET_DV), lambda b, h, n: (rows_b(b, h, n), h)),
        out_shape=jax.ShapeDtypeStruct((T, RET_V), F32),
        scratch_shapes=[pltpu.VMEM((RET_DK, RET_DV), F32)],
        compiler_params=_cparams("parallel", "parallel", "arbitrary"),
        name="retention_bwd",
    )(lg, proj, proj, proj, cos, sin)

    return pl.pallas_call(
        _ret_fwd_body,
        grid=(B, H, N),
        in_specs=[
            smem,
            pl.BlockSpec((C, RET_DK), lambda b, h, n: (b * N + n, h)),
            pl.BlockSpec((C, RET_DK), lambda b, h, n: (b * N + n, kq + h)),
            pl.BlockSpec((C, RET_DV), lambda b, h, n: (b * N + n, kv + h)),
            pl.BlockSpec((C, RET_DV), lambda b, h, n: (b * N + n, kg + h)),
            pl.BlockSpec((C, RET_DK // 2), lambda b, h, n: (n, 0)),
            pl.BlockSpec((C, RET_DK // 2), lambda b, h, n: (n, 0)),
            pl.BlockSpec((C, RET_DV), lambda b, h, n: (b * N + n, h)),
            pl.BlockSpec((1, RET_DV), lambda b, h, n: (0, h)),
        ],
        out_specs=pl.BlockSpec((C, RET_DV), lambda b, h, n: (b * N + n, h)),
        out_shape=jax.ShapeDtypeStruct((T, RET_V), BF16),
        scratch_shapes=[pltpu.VMEM((RET_DK, RET_DV), F32), pltpu.VMEM((C, C), F32)],
        compiler_params=_cparams("parallel", "parallel", "arbitrary"),
        name="retention_fwd",
    )(lg, proj, proj, proj, proj, cos, sin, ob, gn_w.reshape(1, RET_V).astype(F32))


CONV_HALO = 16


def _conv_body(x_ref, p_ref, nx_ref, w_ref, o_ref, ext_ref, *, normalize, q_blocks):
    i = pl.program_id(1)
    c = pl.program_id(2)
    last = pl.num_programs(1) - 1
    ts, tc = x_ref.shape
    prev = jnp.where(i == 0, 0.0, p_ref[...].astype(F32))
    nxt = jnp.where(i == last, 0.0, nx_ref[...].astype(F32))
    ext_ref[0:8, :] = prev[CONV_HALO - 8:, :]
    ext_ref[8:8 + ts, :] = x_ref[...].astype(F32)
    ext_ref[8 + ts:16 + ts, :] = nxt[:8, :]
    w = w_ref[...]
    half = GDN_CONV // 2
    acc = jnp.zeros((ts, tc), F32)
    for t in range(GDN_CONV):
        acc = acc + ext_ref[pl.ds(8 - half + t, ts), :] * w[t:t + 1, :]
    y = acc * jax.nn.sigmoid(acc)
    if normalize:
        scale = jnp.where(c < q_blocks, GDN_DK ** -0.5, 1.0)
        for hh in range(tc // GDN_DK):
            yh = y[:, hh * GDN_DK:(hh + 1) * GDN_DK]
            ss = jnp.sum(yh * yh, axis=-1, keepdims=True)
            o_ref[:, hh * GDN_DK:(hh + 1) * GDN_DK] = (yh * (lax.rsqrt(ss + NORM_EPS) * scale)).astype(o_ref.dtype)
    else:
        o_ref[...] = y.astype(o_ref.dtype)


def _gdn_conv(proj, B, S, col0, n_ch, conv_w, normalize, ts=512, tc=512):
    ts = min(ts, S)
    T = B * S
    nS = S // ts
    cb0 = col0 // tc
    hr = ts // CONV_HALO
    nH = T // CONV_HALO
    body = functools.partial(_conv_body, normalize=normalize, q_blocks=GDN_QK // tc)
    return pl.pallas_call(
        body,
        grid=(B, nS, n_ch // tc),
        in_specs=[
            pl.BlockSpec((ts, tc), lambda b, i, c: (b * nS + i, cb0 + c)),
            pl.BlockSpec((CONV_HALO, tc), lambda b, i, c: (jnp.maximum((b * nS + i) * hr - 1, 0), cb0 + c)),
            pl.BlockSpec((CONV_HALO, tc), lambda b, i, c: (jnp.minimum((b * nS + i + 1) * hr, nH - 1), cb0 + c)),
            pl.BlockSpec((GDN_CONV, tc), lambda b, i, c: (0, c)),
        ],
        out_specs=pl.BlockSpec((ts, tc), lambda b, i, c: (b * nS + i, c)),
        out_shape=jax.ShapeDtypeStruct((T, n_ch), BF16),
        scratch_shapes=[pltpu.VMEM((ts + 16, tc), F32)],
        compiler_params=_cparams("parallel", "parallel", "parallel"),
        name="gdn_conv",
    )(proj, proj, proj, conv_w.astype(F32))


def _softplus(x):
    return jnp.maximum(x, 0.0) + jnp.log1p(jnp.exp(-jnp.abs(x)))


def _gdn_body(q_ref, k_ref, v_ref, col_ref, row_ref, pcol_ref, prow_ref, *rest, reverse, final):
    if final:
        ob_ref, z_ref, nw_ref, o_ref, s_ref = rest
    else:
        o_ref, s_ref = rest
    n = pl.program_id(2)
    C = q_ref.shape[0]
    HB = GDN_HB

    @pl.when(n == 0)
    def _():
        s_ref[...] = jnp.zeros_like(s_ref)

    ii = lax.broadcasted_iota(jnp.int32, (C, C), 0)
    jj = lax.broadcasted_iota(jnp.int32, (C, C), 1)
    if reverse:
        tri, strict = ii <= jj, ii < jj
    else:
        tri, strict = ii >= jj, ii > jj
    tri_f = tri.astype(F32)
    eye = (ii == jj).astype(F32)
    end = 0 if reverse else C - 1

    col = col_ref[0]
    row = row_ref[0, 0]
    pcol = pcol_ref[0]
    prow = prow_ref[0]
    beta_col = jax.nn.sigmoid(col)
    g_col = -jnp.exp(pcol[0:1, :]) * _softplus(col + pcol[1:2, :])
    g_row = -jnp.exp(prow[:, 0:1]) * _softplus(row + prow[:, 1:2])
    hi = lax.Precision.HIGHEST
    dec_col = jnp.dot(tri_f, g_col, precision=hi, preferred_element_type=F32)
    dec_row = lax.dot_general(g_row, tri_f, (((1,), (1,)), ((), ())), precision=hi,
                              preferred_element_type=F32)

    heads = range(HB)
    St = [s_ref[hh] for hh in heads]
    Sb = [s.astype(BF16) for s in St]
    qf, kf, qkt, kkt = [], [], [], []
    for pair in range(HB // 2):
        qh = q_ref[:, pair * GDN_DK:(pair + 1) * GDN_DK]
        kh = k_ref[:, pair * GDN_DK:(pair + 1) * GDN_DK]
        qk = jnp.concatenate([qh, kh], axis=0)
        qkk = lax.dot_general(qk, kh, (((1,), (1,)), ((), ())), preferred_element_type=F32)
        qkt.append(qkk[:C])
        kkt.append(qkk[C:])
        kf.append(kh.astype(F32))
        qf.append(qh.astype(F32))
    bcol = [beta_col[:, hh:hh + 1] for hh in heads]
    dcol = [dec_col[:, HB + hh:HB + hh + 1] for hh in heads]
    drow = [dec_row[HB + hh:HB + hh + 1, :] for hh in heads]
    dend = [d[end:end + 1, :] for d in dcol]
    L = [jnp.where(tri, jnp.exp(jnp.where(tri, dcol[hh] - drow[hh], 0.0)), 0.0) for hh in heads]
    ecol = [jnp.exp(d) for d in dcol]
    P = [jnp.where(strict, -(kkt[hh // 2] * L[hh] * bcol[hh]), 0.0) for hh in heads]
    Tm = [eye + p for p in P]
    for _ in range(max(1, int(math.ceil(math.log2(C))) - 1)):
        Pb = [p.astype(BF16) for p in P]
        P = [jnp.dot(pb, pb, preferred_element_type=F32) for pb in Pb]
        Tm = [t + jnp.dot(t.astype(BF16), p.astype(BF16), preferred_element_type=F32) for t, p in zip(Tm, P)]
    sol = []
    for hh in heads:
        vh = v_ref[:, hh * GDN_DV:(hh + 1) * GDN_DV].astype(F32)
        rhs = jnp.concatenate([vh * bcol[hh], kf[hh // 2] * (bcol[hh] * ecol[hh])], axis=-1).astype(BF16)
        sol.append(jnp.dot(Tm[hh].astype(BF16), rhs, preferred_element_type=F32))
    vnb = [(sol[hh][:, :GDN_DV] - jnp.dot(sol[hh][:, GDN_DV:].astype(BF16), Sb[hh], preferred_element_type=F32)
            ).astype(BF16) for hh in heads]
    outs = [jnp.dot((qf[hh // 2] * ecol[hh]).astype(BF16), Sb[hh], preferred_element_type=F32)
            + jnp.dot((qkt[hh // 2] * L[hh]).astype(BF16), vnb[hh], preferred_element_type=F32) for hh in heads]
    s_new = []
    for hh in heads:
        k_tail = (kf[hh // 2] * jnp.exp(dend[hh] - dcol[hh])).astype(BF16)
        s_new.append(St[hh] * jnp.exp(dend[hh]) + lax.dot_general(
            k_tail, vnb[hh], (((0,), (0,)), ((), ())), preferred_element_type=F32))
    for hh in heads:
        s_ref[hh] = s_new[hh]
    for hh in heads:
        sl = slice(hh * GDN_DV, (hh + 1) * GDN_DV)
        o = outs[hh]
        if final:
            o = o + ob_ref[:, sl]
            y = o * lax.rsqrt(jnp.mean(o * o, axis=-1, keepdims=True) + NORM_EPS) * nw_ref[...]
            z = z_ref[:, sl].astype(F32)
            o_ref[:, sl] = (y * (z * jax.nn.sigmoid(z))).astype(o_ref.dtype)
        else:
            o_ref[:, sl] = o


def _gdn_side_inputs(ba, a_log, dt_bias, B, S, C, d):
    T = B * S
    N = S // C
    Hv, HB = GDN_V_HEADS, GDN_HB
    nhb = Hv // HB
    bl = ba[:, d * Hv:(d + 1) * Hv].reshape(T, nhb, HB)
    al = ba[:, 2 * Hv + d * Hv:2 * Hv + (d + 1) * Hv].reshape(T, nhb, HB)
    col = jnp.concatenate([bl, al], axis=-1).transpose(1, 0, 2)
    row = col.reshape(nhb, B * N, C, 2 * HB).transpose(0, 1, 3, 2)
    zeros = jnp.zeros((nhb, HB), F32)
    p0 = jnp.concatenate([zeros, a_log[d].astype(F32).reshape(nhb, HB)], axis=-1)
    p1 = jnp.concatenate([zeros, dt_bias[d].astype(F32).reshape(nhb, HB)], axis=-1)
    pcol = jnp.stack([p0, p1], axis=1)
    prow = jnp.stack([p0, p1], axis=2)
    return col, row, pcol, prow


def _gdn_direction(qk_c, v_c, ba, a_log, dt_bias, B, S, d, final_inputs=None):
    C = min(GDN_CHUNK, S)
    N = S // C
    T = B * S
    HB = GDN_HB
    nhb = GDN_V_HEADS // HB
    reverse = d == 1
    final = final_inputs is not None
    col, row, pcol, prow = _gdn_side_inputs(ba, a_log, dt_bias, B, S, C, d)
    qw = (HB // 2) * GDN_DK
    vw = HB * GDN_DV
    k_off = GDN_QK // qw

    def ch(n):
        return (N - 1 - n) if reverse else n

    in_specs = [
        pl.BlockSpec((C, qw), lambda b, hb, n: (b * N + ch(n), hb)),
        pl.BlockSpec((C, qw), lambda b, hb, n: (b * N + ch(n), k_off + hb)),
        pl.BlockSpec((C, vw), lambda b, hb, n: (b * N + ch(n), hb)),
        pl.BlockSpec((1, C, 2 * HB), lambda b, hb, n: (hb, b * N + ch(n), 0)),
        pl.BlockSpec((1, 1, 2 * HB, C), lambda b, hb, n: (hb, b * N + ch(n), 0, 0)),
        pl.BlockSpec((1, 2, 2 * HB), lambda b, hb, n: (hb, 0, 0)),
        pl.BlockSpec((1, 2 * HB, 2), lambda b, hb, n: (hb, 0, 0)),
    ]
    args = [qk_c, qk_c, v_c, col, row, pcol, prow]
    if final:
        o_other, proj, z_col0, norm_w = final_inputs
        zb = z_col0 // vw
        in_specs += [
            pl.BlockSpec((C, vw), lambda b, hb, n: (b * N + ch(n), hb)),
            pl.BlockSpec((C, vw), lambda b, hb, n: (b * N + ch(n), zb + hb)),
            pl.BlockSpec((1, GDN_DV), lambda b, hb, n: (0, 0)),
        ]
        args += [o_other, proj, norm_w.reshape(1, GDN_DV).astype(F32)]
    return pl.pallas_call(
        functools.partial(_gdn_body, reverse=reverse, final=final),
        grid=(B, nhb, N),
        in_specs=in_specs,
        out_specs=pl.BlockSpec((C, vw), lambda b, hb, n: (b * N + ch(n), hb)),
        out_shape=jax.ShapeDtypeStruct((T, GDN_V), BF16 if final else F32),
        scratch_shapes=[pltpu.VMEM((HB, GDN_DK, GDN_DV), F32)],
        compiler_params=_cparams("parallel", "parallel", "arbitrary"),
        name="gdn_final" if final else "gdn_first",
    )(*args)


def _merge_body(a0_ref, a1_ref, w0_ref, w1_ref, g0_ref, g1_ref, o_ref):
    p0 = jnp.dot(a0_ref[...], w0_ref[0], preferred_element_type=F32)
    p1 = jnp.dot(a1_ref[...], w1_ref[0], preferred_element_type=F32)
    g0 = jax.nn.sigmoid(g0_ref[...].astype(F32))
    g1 = jax.nn.sigmoid(g1_ref[...].astype(F32))
    o_ref[...] = (g0 * p0 + g1 * p1).astype(o_ref.dtype)


def _merge(ret, gdn, w_branch, gate_logits, tm=512, tn=512):
    T, K = ret.shape
    D = w_branch.shape[-1]
    tm = min(tm, T)
    nj = D // tn
    return pl.pallas_call(
        _merge_body,
        grid=(T // tm, nj),
        in_specs=[
            pl.BlockSpec((tm, K), lambda i, j: (i, 0)),
            pl.BlockSpec((tm, K), lambda i, j: (i, 0)),
            pl.BlockSpec((1, K, tn), lambda i, j: (0, 0, j)),
            pl.BlockSpec((1, K, tn), lambda i, j: (1, 0, j)),
            pl.BlockSpec((tm, tn), lambda i, j: (i, j)),
            pl.BlockSpec((tm, tn), lambda i, j: (i, nj + j)),
        ],
        out_specs=pl.BlockSpec((tm, tn), lambda i, j: (i, j)),
        out_shape=jax.ShapeDtypeStruct((T, D), BF16),
        compiler_params=_cparams("parallel", "arbitrary"),
        name="branch_merge",
    )(ret, gdn, w_branch, w_branch, gate_logits, gate_logits)


def _xattn_body(q_ref, k_ref, v_ref, o_ref):
    scale = XA_DH ** -0.5
    for h in range(XA_HEADS):
        sl = slice(h * XA_DH, (h + 1) * XA_DH)
        s = lax.dot_general(q_ref[:, sl], k_ref[:, sl], (((1,), (1,)), ((), ())),
                            preferred_element_type=F32) * scale
        m = jnp.max(s, axis=-1, keepdims=True)
        e = jnp.exp(s - m)
        p = e / jnp.sum(e, axis=-1, keepdims=True)
        o_ref[:, sl] = jnp.dot(p.astype(BF16), v_ref[:, sl], preferred_element_type=F32).astype(o_ref.dtype)


def _xattn(q, kv, B, S, M, tq=512):
    T, D = q.shape
    tq = min(tq, S)
    nq = S // tq
    return pl.pallas_call(
        _xattn_body,
        grid=(B, nq),
        in_specs=[
            pl.BlockSpec((tq, D), lambda b, i: (b * nq + i, 0)),
            pl.BlockSpec((M, D), lambda b, i: (b, 0)),
            pl.BlockSpec((M, D), lambda b, i: (b, 1)),
        ],
        out_specs=pl.BlockSpec((tq, D), lambda b, i: (b * nq + i, 0)),
        out_shape=jax.ShapeDtypeStruct((T, D), BF16),
        compiler_params=_cparams("parallel", "arbitrary"),
        name="cross_attention",
    )(q, kv, kv)


META_IDX, META_GATE, META_RANK = 0, TOP_K, 2 * TOP_K


def _router_body(x_ref, g_ref, w_ref, b_ref, h_ref, meta_ref, cnt_ref, carry_ref):
    i = pl.program_id(0)
    tm = x_ref.shape[0]

    @pl.when(i == 0)
    def _():
        carry_ref[...] = jnp.zeros_like(carry_ref)

    x = x_ref[...]
    h = x * lax.rsqrt(jnp.mean(x * x, axis=-1, keepdims=True) + NORM_EPS) * g_ref[...]
    h_ref[...] = h
    logits = jnp.dot(h, w_ref[...], precision=lax.Precision.HIGHEST, preferred_element_type=F32) + b_ref[...]
    lane = lax.broadcasted_iota(jnp.int32, (tm, LANES), 1)
    neg = jnp.float32(-jnp.inf)
    logits = jnp.where(lane < N_EXPERTS, logits, neg)
    vals, hots = [], []
    for _ in range(TOP_K):
        m = jnp.max(logits, axis=-1, keepdims=True)
        first = jnp.min(jnp.where(logits == m, lane, LANES), axis=-1, keepdims=True)
        hot = lane == first
        vals.append(m)
        hots.append(hot)
        logits = jnp.where(hot, neg, logits)
    es = [jnp.exp(v - vals[0]) for v in vals]
    den = es[0]
    for e in es[1:]:
        den = den + e
    mh = jnp.zeros((tm, LANES), F32)
    for hot in hots:
        mh = mh + hot.astype(F32)
    ii = lax.broadcasted_iota(jnp.int32, (tm, tm), 0)
    jj = lax.broadcasted_iota(jnp.int32, (tm, tm), 1)
    before = (ii > jj).astype(BF16)
    ranks = jnp.dot(before, mh.astype(BF16), preferred_element_type=F32) + carry_ref[...]
    lane_f = lane.astype(F32)
    meta = jnp.zeros((tm, LANES), F32)
    for kk in range(TOP_K):
        e_k = jnp.sum(jnp.where(hots[kk], lane_f, 0.0), axis=-1, keepdims=True)
        r_k = jnp.sum(jnp.where(hots[kk], ranks, 0.0), axis=-1, keepdims=True)
        meta = meta + jnp.where(lane == META_IDX + kk, e_k, 0.0)
        meta = meta + jnp.where(lane == META_GATE + kk, es[kk] / den, 0.0)
        meta = meta + jnp.where(lane == META_RANK + kk, r_k, 0.0)
    meta_ref[...] = meta
    carry_ref[...] = carry_ref[...] + jnp.sum(mh, axis=0, keepdims=True)
    cnt_ref[...] = carry_ref[...]


def _router(x, gain, w_router, b_router, tm=512):
    T, D = x.shape
    tm = min(tm, T)
    wr = jnp.zeros((D, LANES), F32).at[:, :N_EXPERTS].set(w_router.astype(F32))
    br = jnp.zeros((1, LANES), F32).at[0, :N_EXPERTS].set(b_router.astype(F32))
    return pl.pallas_call(
        _router_body,
        grid=(T // tm,),
        in_specs=[
            pl.BlockSpec((tm, D), lambda i: (i, 0)),
            pl.BlockSpec((1, D), lambda i: (0, 0)),
            pl.BlockSpec((D, LANES), lambda i: (0, 0)),
            pl.BlockSpec((1, LANES), lambda i: (0, 0)),
        ],
        out_specs=[
            pl.BlockSpec((tm, D), lambda i: (i, 0)),
            pl.BlockSpec((tm, LANES), lambda i: (i, 0)),
            pl.BlockSpec((1, LANES), lambda i: (0, 0)),
        ],
        out_shape=[
            jax.ShapeDtypeStruct((T, D), F32),
            jax.ShapeDtypeStruct((T, LANES), F32),
            jax.ShapeDtypeStruct((1, LANES), F32),
        ],
        scratch_shapes=[pltpu.VMEM((1, LANES), F32)],
        compiler_params=_cparams("arbitrary"),
        name="moe_router",
    )(x, gain.reshape(1, D).astype(F32), wr, br)


def _dispatch_body(dest_ref, h_ref, init_ref, xs_ref, sem):
    del init_ref
    tm = h_ref.shape[0]

    def row_copy(r, kk):
        return pltpu.make_async_copy(h_ref.at[pl.ds(r, 1)], xs_ref.at[pl.ds(dest_ref[r * TOP_K + kk], 1)], sem)

    def issue(r, carry):
        for kk in range(TOP_K):
            row_copy(r, kk).start()
        return carry

    def drain(r, carry):
        for kk in range(TOP_K):
            row_copy(r, kk).wait()
        return carry

    lax.fori_loop(0, tm, issue, 0)
    lax.fori_loop(0, tm, drain, 0)


def _dispatch(h, dest_flat, n_slots, tm=256):
    T, D = h.shape
    tm = min(tm, T)
    init = jnp.zeros((n_slots, D), F32)
    return pl.pallas_call(
        _dispatch_body,
        grid=(T // tm,),
        in_specs=[
            pl.BlockSpec((tm * TOP_K,), lambda i: (i,), memory_space=pltpu.SMEM),
            pl.BlockSpec((tm, D), lambda i: (i, 0)),
            pl.BlockSpec(memory_space=pl.ANY),
        ],
        out_specs=pl.BlockSpec(memory_space=pl.ANY),
        out_shape=jax.ShapeDtypeStruct((n_slots, D), F32),
        scratch_shapes=[pltpu.SemaphoreType.DMA(())],
        input_output_aliases={2: 0},
        compiler_params=_cparams("arbitrary"),
        name="moe_dispatch",
    )(dest_flat, h, init)


def _expert_body(be_ref, nb_ref, x_ref, wg_ref, wu_ref, bg_ref, bu_ref, wd_ref, bd_ref, o_ref, xb_ref, acc_ref):
    blk = pl.program_id(0)
    j = pl.program_id(1)
    nj = pl.num_programs(1)
    used = blk < nb_ref[0]

    @pl.when(used)
    def _():
        @pl.when(j == 0)
        def _():
            xb_ref[...] = x_ref[...].astype(BF16)

        xb = xb_ref[...]
        g = jnp.dot(xb, wg_ref[0], preferred_element_type=F32) + bg_ref[0]
        u = jnp.dot(xb, wu_ref[0], preferred_element_type=F32) + bu_ref[0]
        g = jnp.minimum(g, SWIGLU_LIMIT)
        u = jnp.clip(u, -SWIGLU_LIMIT, SWIGLU_LIMIT)
        act = g * jax.nn.sigmoid(SWIGLU_ALPHA * g) * (u + 1.0)
        part = jnp.dot(act.astype(BF16), wd_ref[0], preferred_element_type=F32)

        @pl.when(j == 0)
        def _():
            acc_ref[...] = part

        @pl.when(j > 0)
        def _():
            acc_ref[...] = acc_ref[...] + part

        @pl.when(j == nj - 1)
        def _():
            o_ref[...] = acc_ref[...] + bd_ref[0]

    @pl.when(jnp.logical_and(jnp.logical_not(used), j == nj - 1))
    def _():
        o_ref[...] = jnp.zeros_like(o_ref)


def _experts(xs, block_e, nb_used, wg, wu, bg, bu, wd, bd, tf=512):
    n_slots, D = xs.shape
    NB = n_slots // MOE_BLOCK
    nj = D_FF // tf

    def xrow(blk, j, be, nb):
        return (jnp.minimum(blk, nb[0] - 1), 0)

    grid_spec = pltpu.PrefetchScalarGridSpec(
        num_scalar_prefetch=2,
        grid=(NB, nj),
        in_specs=[
            pl.BlockSpec((MOE_BLOCK, D), xrow),
            pl.BlockSpec((1, D, tf), lambda blk, j, be, nb: (be[blk], 0, j)),
            pl.BlockSpec((1, D, tf), lambda blk, j, be, nb: (be[blk], 0, j)),
            pl.BlockSpec((1, 1, tf), lambda blk, j, be, nb: (be[blk], 0, j)),
            pl.BlockSpec((1, 1, tf), lambda blk, j, be, nb: (be[blk], 0, j)),
            pl.BlockSpec((1, tf, D), lambda blk, j, be, nb: (be[blk], j, 0)),
            pl.BlockSpec((1, 1, D), lambda blk, j, be, nb: (be[blk], 0, 0)),
        ],
        out_specs=pl.BlockSpec((MOE_BLOCK, D), lambda blk, j, be, nb: (blk, 0)),
        scratch_shapes=[pltpu.VMEM((MOE_BLOCK, D), BF16), pltpu.VMEM((MOE_BLOCK, D), F32)],
    )
    return pl.pallas_call(
        _expert_body,
        grid_spec=grid_spec,
        out_shape=jax.ShapeDtypeStruct((n_slots, D), F32),
        compiler_params=_cparams("arbitrary", "arbitrary"),
        name="moe_experts",
    )(block_e, nb_used, xs, wg, wu, bg, bu, wd, bd)


def _combine_body(dest_ref, ys_ref, meta_ref, x_ref, g_ref, o_ref, buf_ref, sem):
    tm = x_ref.shape[0]

    def row_copy(r, kk):
        return pltpu.make_async_copy(ys_ref.at[pl.ds(dest_ref[r * TOP_K + kk], 1)],
                                     buf_ref.at[kk, pl.ds(r, 1)], sem)

    def issue(r, carry):
        for kk in range(TOP_K):
            row_copy(r, kk).start()
        return carry

    def drain(r, carry):
        for kk in range(TOP_K):
            row_copy(r, kk).wait()
        return carry

    lax.fori_loop(0, tm, issue, 0)
    lax.fori_loop(0, tm, drain, 0)
    meta = meta_ref[...]
    y = x_ref[...]
    for kk in range(TOP_K):
        y = y + buf_ref[kk] * meta[:, META_GATE + kk:META_GATE + kk + 1]
    o_ref[...] = y * lax.rsqrt(jnp.mean(y * y, axis=-1, keepdims=True) + NORM_EPS) * g_ref[...]


def _combine(ys, dest_flat, meta, x, gain, tm=256):
    T, D = x.shape
    tm = min(tm, T)
    return pl.pallas_call(
        _combine_body,
        grid=(T // tm,),
        in_specs=[
            pl.BlockSpec((tm * TOP_K,), lambda i: (i,), memory_space=pltpu.SMEM),
            pl.BlockSpec(memory_space=pl.ANY),
            pl.BlockSpec((tm, LANES), lambda i: (i, 0)),
            pl.BlockSpec((tm, D), lambda i: (i, 0)),
            pl.BlockSpec((1, D), lambda i: (0, 0)),
        ],
        out_specs=pl.BlockSpec((tm, D), lambda i: (i, 0)),
        out_shape=jax.ShapeDtypeStruct((T, D), F32),
        scratch_shapes=[pltpu.VMEM((TOP_K, tm, D), F32), pltpu.SemaphoreType.DMA(())],
        compiler_params=_cparams("arbitrary"),
        name="moe_combine",
    )(dest_flat, ys, meta, x, gain.reshape(1, D).astype(F32))


def _moe_final(x2, ln_moe, w_router, b_router, ew, ln_final):
    T, D = x2.shape
    A = T * TOP_K
    NB = A // MOE_BLOCK + N_EXPERTS
    h, meta, cnt = _router(x2, ln_moe, w_router, b_router)
    idx = meta[:, META_IDX:META_IDX + TOP_K].astype(jnp.int32)
    rank = meta[:, META_RANK:META_RANK + TOP_K].astype(jnp.int32)
    counts = cnt[0, :N_EXPERTS].astype(jnp.int32)
    padded = (counts + MOE_BLOCK - 1) // MOE_BLOCK * MOE_BLOCK
    pad_end = jnp.cumsum(padded)
    pad_start = pad_end - padded
    dest = (pad_start[idx] + rank).reshape(A)
    block_e = jnp.minimum(jnp.searchsorted(pad_end, jnp.arange(NB, dtype=jnp.int32) * MOE_BLOCK, side='right'),
                          N_EXPERTS - 1).astype(jnp.int32)
    nb_used = (pad_end[-1:] // MOE_BLOCK).astype(jnp.int32)
    xs = _dispatch(h, dest, NB * MOE_BLOCK)
    ys = _experts(xs, block_e, nb_used, *ew)
    return _combine(ys, dest, meta, x2, ln_final)


def _prepare_weights(w_in, w_branch, w_out, xa_w_q, xa_w_kv, xa_w_o, w_gate_up, b_gate_up, w_down, b_down):
    main_w = 2 * RET_QK + 2 * RET_V + 2 * GDN_QK + 2 * GDN_V
    ba_w = 4 * GDN_V_HEADS
    w = {}
    w['in_main'] = w_in[:, :main_w].astype(BF16)
    w['in_ba'] = w_in[:, main_w:main_w + ba_w].astype(BF16)
    w['in_gate'] = w_in[:, main_w + ba_w:].astype(BF16)
    w['branch'] = w_branch.astype(BF16)
    w['out'] = w_out.astype(BF16)
    w['xa_q'] = xa_w_q.astype(BF16)
    w['xa_kv'] = xa_w_kv.astype(BF16)
    w['xa_o'] = xa_w_o.astype(BF16)
    E = w_gate_up.shape[0]
    gu = w_gate_up.reshape(E, D_MODEL, D_FF, 2)
    bgu = b_gate_up.reshape(E, 1, D_FF, 2).astype(F32)
    w['experts'] = (gu[..., 0].astype(BF16), gu[..., 1].astype(BF16), bgu[..., 0], bgu[..., 1],
                    w_down.astype(BF16), b_down.reshape(E, 1, D_MODEL).astype(F32))
    return w


def _rope_tables(S):
    inv_freq = ROPE_BASE ** (-jnp.arange(0, RET_DK, 2, dtype=F32) / RET_DK)
    ang = jnp.arange(S, dtype=F32)[:, None] * inv_freq[None, :]
    return jnp.cos(ang), jnp.sin(ang)


def _encoder(x, mem, w, p):
    B, S, D = x.shape
    M = mem.shape[1]
    T = B * S
    x0 = x.reshape(T, D)
    h = _rmsnorm(x0, p['ln_mix'], BF16)
    proj = _matmul(h, w['in_main'], BF16, name="in_proj")
    ba = _matmul(h, w['in_ba'], F32, name="in_proj_ba")
    gate_logits = _matmul(h, w['in_gate'], BF16, name="in_proj_gate")
    cos, sin = _rope_tables(S)
    ret = _retention(proj, B, S, p['ret_log_decay'], p['ret_gn_w'], cos, sin)
    g0 = 2 * RET_QK + 2 * RET_V
    qk_c = _gdn_conv(proj, B, S, g0, 2 * GDN_QK, p['gdn_conv_w'][:, :2 * GDN_QK], True)
    v_c = _gdn_conv(proj, B, S, g0 + 2 * GDN_QK, GDN_V, p['gdn_conv_w'][:, 2 * GDN_QK:], False)
    o_b = _gdn_direction(qk_c, v_c, ba, p['gdn_a_log'], p['gdn_dt_bias'], B, S, 1)
    gdn = _gdn_direction(qk_c, v_c, ba, p['gdn_a_log'], p['gdn_dt_bias'], B, S, 0,
                         final_inputs=(o_b, proj, g0 + 2 * GDN_QK + GDN_V, p['gdn_norm_w']))
    merged = _merge(ret, gdn, w['branch'], gate_logits)
    x1 = _matmul(merged, w['out'], F32, residual=x0, name="mixer_out")
    hq = _rmsnorm(x1, p['ln_xa'], BF16)
    hm = _rmsnorm(mem.reshape(B * M, D), p['ln_mem'], BF16)
    q = _matmul(hq, w['xa_q'], BF16, name="xa_q")
    kv = _matmul(hm, w['xa_kv'], BF16, name="xa_kv")
    att = _xattn(q, kv, B, S, M)
    x2 = _matmul(att, w['xa_o'], F32, residual=x1, name="xa_out")
    y = _moe_final(x2, p['ln_moe'], p['w_router'], p['b_router'], w['experts'], p['ln_final'])
    return y.reshape(B, S, D)


def kernel(x_prompt, x_sample, mem_prompt, mem_sample, ln_mix, w_in, ret_log_decay, ret_gn_w, gdn_conv_w, gdn_a_log, gdn_dt_bias, gdn_norm_w, w_branch, w_out, ln_xa, ln_mem, xa_w_q, xa_w_kv, xa_w_o, ln_moe, w_router, b_router, w_gate_up, b_gate_up, w_down, b_down, ln_final):
    depth = w_in.shape[0]
    assert depth == 1, "single-layer trunk"
    l = 0
    w = _prepare_weights(w_in[l], w_branch[l], w_out[l], xa_w_q[l], xa_w_kv[l], xa_w_o[l],
                         w_gate_up[l], b_gate_up[l], w_down[l], b_down[l])
    p = dict(ln_mix=ln_mix[l], ret_log_decay=ret_log_decay[l], ret_gn_w=ret_gn_w[l], gdn_conv_w=gdn_conv_w[l],
             gdn_a_log=gdn_a_log[l], gdn_dt_bias=gdn_dt_bias[l], gdn_norm_w=gdn_norm_w[l], ln_xa=ln_xa[l],
             ln_mem=ln_mem[l], ln_moe=ln_moe[l], w_router=w_router[l], b_router=b_router[l], ln_final=ln_final)
    y_prompt = _encoder(x_prompt, mem_prompt, w, p)
    y_sample = _encoder(x_sample, mem_sample, w, p)
    return (y_prompt, y_sample)
```

```python
import functools
import math

import jax
import jax.numpy as jnp
from jax import lax
from jax.experimental import pallas as pl
from jax.experimental.pallas import tpu as pltpu

F32 = jnp.float32
BF16 = jnp.bfloat16

D_MODEL = 2048
RET_HEADS = 8
RET_DK = 256
RET_DV = 512
ROPE_BASE = 10000.0
GDN_QK_HEADS = 16
GDN_V_HEADS = 32
GDN_DK = 128
GDN_DV = 128
GDN_CONV = 5
XA_HEADS = 4
XA_DH = D_MODEL // XA_HEADS
N_EXPERTS = 32
TOP_K = 4
D_FF = 2048
SWIGLU_LIMIT = 7.0
SWIGLU_ALPHA = 1.702
MOE_BLOCK = 512
NORM_EPS = 1e-6
N_BRANCH = 2
RET_QK = RET_HEADS * RET_DK
RET_V = RET_HEADS * RET_DV
GDN_QK = GDN_QK_HEADS * GDN_DK
GDN_V = GDN_V_HEADS * GDN_DV

LANES = 128
VMEM_LIMIT = 56 * 1024 * 1024

RET_CHUNK = 256
RET_HB = 2
GDN_CHUNK = 128
GDN_HB = 16


def _cparams(*sem):
    return pltpu.CompilerParams(dimension_semantics=sem, vmem_limit_bytes=VMEM_LIMIT)


def _rmsnorm_body(x_ref, g_ref, o_ref):
    x = x_ref[...].astype(F32)
    ms = jnp.mean(x * x, axis=-1, keepdims=True)
    o_ref[...] = (x * lax.rsqrt(ms + NORM_EPS) * g_ref[...]).astype(o_ref.dtype)


def _rmsnorm(x, gain, out_dtype, tm=512):
    T, D = x.shape
    tm = min(tm, T)
    return pl.pallas_call(
        _rmsnorm_body,
        grid=(T // tm,),
        in_specs=[pl.BlockSpec((tm, D), lambda i: (i, 0)), pl.BlockSpec((1, D), lambda i: (0, 0))],
        out_specs=pl.BlockSpec((tm, D), lambda i: (i, 0)),
        out_shape=jax.ShapeDtypeStruct((T, D), out_dtype),
        compiler_params=_cparams("parallel"),
        name="rmsnorm",
    )(x, gain.reshape(1, D).astype(F32))


def _mm_body(a_ref, b_ref, o_ref):
    o_ref[...] = jnp.dot(a_ref[...], b_ref[...], preferred_element_type=F32).astype(o_ref.dtype)


def _mm_res_body(a_ref, b_ref, r_ref, o_ref):
    o_ref[...] = r_ref[...] + jnp.dot(a_ref[...], b_ref[...], preferred_element_type=F32)


def _matmul(a, b, out_dtype, residual=None, tm=1024, tn=1024, name="matmul"):
    M, K = a.shape
    N = b.shape[1]
    tm = min(tm, M)
    tn = min(tn, N)
    in_specs = [pl.BlockSpec((tm, K), lambda i, j: (i, 0)), pl.BlockSpec((K, tn), lambda i, j: (0, j))]
    args = [a, b]
    body = _mm_body
    if residual is not None:
        in_specs.append(pl.BlockSpec((tm, tn), lambda i, j: (i, j)))
        args.append(residual)
        body = _mm_res_body
    return pl.pallas_call(
        body,
        grid=(M // tm, N // tn),
        in_specs=in_specs,
        out_specs=pl.BlockSpec((tm, tn), lambda i, j: (i, j)),
        out_shape=jax.ShapeDtypeStruct((M, N), out_dtype),
        compiler_params=_cparams("parallel", "arbitrary"),
        name=name,
    )(*args)


def _rotary(x, cos, sin):
    half = x.shape[-1] // 2
    x1, x2 = x[:, :half], x[:, half:]
    return jnp.concatenate([x1 * cos - x2 * sin, x1 * sin + x2 * cos], axis=-1)


def _ret_bwd_body(lg_ref, q_ref, k_ref, v_ref, cos_ref, sin_ref, ob_ref, r_ref):
    h = pl.program_id(1)
    n = pl.program_id(2)
    C = q_ref.shape[0]

    @pl.when(n == 0)
    def _():
        r_ref[...] = jnp.zeros_like(r_ref)

    cos, sin = cos_ref[...], sin_ref[...]
    idx = lax.broadcasted_iota(jnp.int32, (C, 1), 0).astype(F32)
    r_old = [r_ref[hh] for hh in range(RET_HB)]
    r_new = []
    for hh in range(RET_HB):
        lg = -jnp.abs(lg_ref[1, h * RET_HB + hh])
        qs = slice(hh * RET_DK, (hh + 1) * RET_DK)
        vs = slice(hh * RET_DV, (hh + 1) * RET_DV)
        q = _rotary(q_ref[:, qs].astype(F32), cos, sin)
        k = _rotary(k_ref[:, qs].astype(F32), cos, sin) * (RET_DK ** -0.5)
        qd = (q * jnp.exp(lg * (C - 1.0 - idx))).astype(BF16)
        kd = (k * jnp.exp(lg * (idx + 1.0))).astype(BF16)
        ob_ref[:, vs] = jnp.dot(qd, r_old[hh].astype(BF16), preferred_element_type=F32)
        kv = lax.dot_general(kd, v_ref[:, vs], (((0,), (0,)), ((), ())), preferred_element_type=F32)
        r_new.append(r_old[hh] * jnp.exp(lg * C) + kv)
    for hh in range(RET_HB):
        r_ref[hh] = r_new[hh]


def _ret_fwd_body(lg_ref, q_ref, k_ref, v_ref, g_ref, cos_ref, sin_ref, ob_ref, gn_ref, o_ref, r_ref, dmat_ref):
    h = pl.program_id(1)
    n = pl.program_id(2)
    C = q_ref.shape[0]
    lgf = [-jnp.abs(lg_ref[0, h * RET_HB + hh]) for hh in range(RET_HB)]
    lgb = [-jnp.abs(lg_ref[1, h * RET_HB + hh]) for hh in range(RET_HB)]

    @pl.when(n == 0)
    def _():
        r_ref[...] = jnp.zeros_like(r_ref)
        ii = lax.broadcasted_iota(jnp.int32, (C, C), 0)
        jj = lax.broadcasted_iota(jnp.int32, (C, C), 1)
        d = (ii - jj).astype(F32)
        for hh in range(RET_HB):
            dmat_ref[hh] = jnp.where(d >= 0, jnp.exp(lgf[hh] * jnp.maximum(d, 0.0)),
                                     jnp.exp(lgb[hh] * jnp.maximum(-d, 0.0)))

    cos, sin = cos_ref[...], sin_ref[...]
    idx = lax.broadcasted_iota(jnp.int32, (C, 1), 0).astype(F32)
    r_old = [r_ref[hh] for hh in range(RET_HB)]
    r_new = []
    for hh in range(RET_HB):
        qs = slice(hh * RET_DK, (hh + 1) * RET_DK)
        vs = slice(hh * RET_DV, (hh + 1) * RET_DV)
        q = _rotary(q_ref[:, qs].astype(F32), cos, sin)
        k = _rotary(k_ref[:, qs].astype(F32), cos, sin) * (RET_DK ** -0.5)
        v = v_ref[:, vs]
        qb = q.astype(BF16)
        kb = k.astype(BF16)
        s = lax.dot_general(qb, kb, (((1,), (1,)), ((), ())), preferred_element_type=F32) * dmat_ref[hh]
        r = r_old[hh]
        qd = (q * jnp.exp(lgf[hh] * (idx + 1.0))).astype(BF16)
        o = (jnp.dot(s.astype(BF16), v, preferred_element_type=F32)
             + jnp.dot(qd, r.astype(BF16), preferred_element_type=F32)
             + ob_ref[:, vs])
        kd = (k * jnp.exp(lgf[hh] * (C - 1.0 - idx))).astype(BF16)
        kv = lax.dot_general(kd, v, (((0,), (0,)), ((), ())), preferred_element_type=F32)
        r_new.append(r * jnp.exp(lgf[hh] * C) + kv)

        mu = jnp.mean(o, axis=-1, keepdims=True)
        oc = o - mu
        var = jnp.mean(oc * oc, axis=-1, keepdims=True)
        y = oc * lax.rsqrt(var + NORM_EPS) * gn_ref[:, vs]
        g = g_ref[:, vs].astype(F32)
        o_ref[:, vs] = (y * (g * jax.nn.sigmoid(g))).astype(o_ref.dtype)
    for hh in range(RET_HB):
        r_ref[hh] = r_new[hh]


def _retention(proj, B, S, log_decay, gn_w, cos, sin):
    C = min(RET_CHUNK, S)
    N = S // C
    T = B * S
    H = RET_HEADS // RET_HB
    qw = RET_HB * RET_DK
    vw = RET_HB * RET_DV
    kq = RET_QK // qw
    kv = (2 * RET_QK) // vw
    kg = kv + H
    smem = pl.BlockSpec(memory_space=pltpu.SMEM)
    lg = log_decay.astype(F32)

    def rows_b(b, h, n):
        return b * N + (N - 1 - n)

    ob = pl.pallas_call(
        _ret_bwd_body,
        grid=(B, H, N),
        in_specs=[
            smem,
            pl.BlockSpec((C, qw), lambda b, h, n: (rows_b(b, h, n), h)),
            pl.BlockSpec((C, qw), lambda b, h, n: (rows_b(b, h, n), kq + h)),
            pl.BlockSpec((C, vw), lambda b, h, n: (rows_b(b, h, n), kv + h)),
            pl.BlockSpec((C, RET_DK // 2), lambda b, h, n: (N - 1 - n, 0)),
            pl.BlockSpec((C, RET_DK // 2), lambda b, h, n: (N - 1 - n, 0)),
        ],
        out_specs=pl.BlockSpec((C, vw), lambda b, h, n: (rows_b(b, h, n), h)),
        out_shape=jax.ShapeDtypeStruct((T, RET_V), F32),
        scratch_shapes=[pltpu.VMEM((RET_HB, RET_DK, RET_DV), F32)],
        compiler_params=_cparams("parallel", "parallel", "arbitrary"),
        name="retention_bwd",
    )(lg, proj, proj, proj, cos, sin)

    return pl.pallas_call(
        _ret_fwd_body,
        grid=(B, H, N),
        in_specs=[
            smem,
            pl.BlockSpec((C, qw), lambda b, h, n: (b * N + n, h)),
            pl.BlockSpec((C, qw), lambda b, h, n: (b * N + n, kq + h)),
            pl.BlockSpec((C, vw), lambda b, h, n: (b * N + n, kv + h)),
            pl.BlockSpec((C, vw), lambda b, h, n: (b * N + n, kg + h)),
            pl.BlockSpec((C, RET_DK // 2), lambda b, h, n: (n, 0)),
            pl.BlockSpec((C, RET_DK // 2), lambda b, h, n: (n, 0)),
            pl.BlockSpec((C, vw), lambda b, h, n: (b * N + n, h)),
            pl.BlockSpec((1, vw), lambda b, h, n: (0, h)),
        ],
        out_specs=pl.BlockSpec((C, vw), lambda b, h, n: (b * N + n, h)),
        out_shape=jax.ShapeDtypeStruct((T, RET_V), BF16),
        scratch_shapes=[pltpu.VMEM((RET_HB, RET_DK, RET_DV), F32), pltpu.VMEM((RET_HB, C, C), F32)],
        compiler_params=_cparams("parallel", "parallel", "arbitrary"),
        name="retention_fwd",
    )(lg, proj, proj, proj, proj, cos, sin, ob, gn_w.reshape(1, RET_V).astype(F32))


CONV_HALO = 16


def _conv_body(x_ref, p_ref, nx_ref, w_ref, o_ref, ext_ref, *, normalize, q_blocks):
    i = pl.program_id(1)
    c = pl.program_id(2)
    last = pl.num_programs(1) - 1
    ts, tc = x_ref.shape
    prev = jnp.where(i == 0, 0.0, p_ref[...].astype(F32))
    nxt = jnp.where(i == last, 0.0, nx_ref[...].astype(F32))
    ext_ref[0:8, :] = prev[CONV_HALO - 8:, :]
    ext_ref[8:8 + ts, :] = x_ref[...].astype(F32)
    ext_ref[8 + ts:16 + ts, :] = nxt[:8, :]
    w = w_ref[...]
    half = GDN_CONV // 2
    acc = jnp.zeros((ts, tc), F32)
    for t in range(GDN_CONV):
        acc = acc + ext_ref[pl.ds(8 - half + t, ts), :] * w[t:t + 1, :]
    y = acc * jax.nn.sigmoid(acc)
    if normalize:
        scale = jnp.where(c < q_blocks, GDN_DK ** -0.5, 1.0)
        for hh in range(tc // GDN_DK):
            yh = y[:, hh * GDN_DK:(hh + 1) * GDN_DK]
            ss = jnp.sum(yh * yh, axis=-1, keepdims=True)
            o_ref[:, hh * GDN_DK:(hh + 1) * GDN_DK] = (yh * (lax.rsqrt(ss + NORM_EPS) * scale)).astype(o_ref.dtype)
    else:
        o_ref[...] = y.astype(o_ref.dtype)


def _gdn_conv(proj, B, S, col0, n_ch, conv_w, normalize, ts=512, tc=512):
    ts = min(ts, S)
    T = B * S
    nS = S // ts
    cb0 = col0 // tc
    hr = ts // CONV_HALO
    nH = T // CONV_HALO
    body = functools.partial(_conv_body, normalize=normalize, q_blocks=GDN_QK // tc)
    return pl.pallas_call(
        body,
        grid=(B, nS, n_ch // tc),
        in_specs=[
            pl.BlockSpec((ts, tc), lambda b, i, c: (b * nS + i, cb0 + c)),
            pl.BlockSpec((CONV_HALO, tc), lambda b, i, c: (jnp.maximum((b * nS + i) * hr - 1, 0), cb0 + c)),
            pl.BlockSpec((CONV_HALO, tc), lambda b, i, c: (jnp.minimum((b * nS + i + 1) * hr, nH - 1), cb0 + c)),
            pl.BlockSpec((GDN_CONV, tc), lambda b, i, c: (0, c)),
        ],
        out_specs=pl.BlockSpec((ts, tc), lambda b, i, c: (b * nS + i, c)),
        out_shape=jax.ShapeDtypeStruct((T, n_ch), BF16),
        scratch_shapes=[pltpu.VMEM((ts + 16, tc), F32)],
        compiler_params=_cparams("parallel", "parallel", "parallel"),
        name="gdn_conv",
    )(proj, proj, proj, conv_w.astype(F32))


def _softplus(x):
    return jnp.maximum(x, 0.0) + jnp.log1p(jnp.exp(-jnp.abs(x)))


def _gdn_body(q_ref, k_ref, v_ref, col_ref, row_ref, pcol_ref, prow_ref, *rest, reverse, final):
    if final:
        ob_ref, z_ref, nw_ref, o_ref, s_ref = rest
    else:
        o_ref, s_ref = rest
    n = pl.program_id(2)
    C = q_ref.shape[0]
    HB = GDN_HB

    @pl.when(n == 0)
    def _():
        s_ref[...] = jnp.zeros_like(s_ref)

    ii = lax.broadcasted_iota(jnp.int32, (C, C), 0)
    jj = lax.broadcasted_iota(jnp.int32, (C, C), 1)
    if reverse:
        tri, strict = ii <= jj, ii < jj
    else:
        tri, strict = ii >= jj, ii > jj
    tri_f = tri.astype(F32)
    eye = (ii == jj).astype(F32)
    end = 0 if reverse else C - 1

    col = col_ref[0]
    row = row_ref[0, 0]
    pcol = pcol_ref[0]
    prow = prow_ref[0]
    beta_col = jax.nn.sigmoid(col)
    g_col = -jnp.exp(pcol[0:1, :]) * _softplus(col + pcol[1:2, :])
    g_row = -jnp.exp(prow[:, 0:1]) * _softplus(row + prow[:, 1:2])
    hi = lax.Precision.HIGHEST
    dec_col = jnp.dot(tri_f, g_col, precision=hi, preferred_element_type=F32)
    dec_row = lax.dot_general(g_row, tri_f, (((1,), (1,)), ((), ())), precision=hi,
                              preferred_element_type=F32)

    heads = range(HB)
    St = [s_ref[hh] for hh in heads]
    Sb = [s.astype(BF16) for s in St]
    qf, kf, qkt, kkt = [], [], [], []
    for pair in range(HB // 2):
        qh = q_ref[:, pair * GDN_DK:(pair + 1) * GDN_DK]
        kh = k_ref[:, pair * GDN_DK:(pair + 1) * GDN_DK]
        qk = jnp.concatenate([qh, kh], axis=0)
        qkk = lax.dot_general(qk, kh, (((1,), (1,)), ((), ())), preferred_element_type=F32)
        qkt.append(qkk[:C])
        kkt.append(qkk[C:])
        kf.append(kh.astype(F32))
        qf.append(qh.astype(F32))
    bcol = [beta_col[:, hh:hh + 1] for hh in heads]
    dcol = [dec_col[:, HB + hh:HB + hh + 1] for hh in heads]
    drow = [dec_row[HB + hh:HB + hh + 1, :] for hh in heads]
    dend = [d[end:end + 1, :] for d in dcol]
    L = [jnp.where(tri, jnp.exp(jnp.where(tri, dcol[hh] - drow[hh], 0.0)), 0.0) for hh in heads]
    ecol = [jnp.exp(d) for d in dcol]
    Ab = [jnp.where(strict, kkt[hh // 2] * L[hh] * bcol[hh], 0.0).astype(BF16) for hh in heads]
    levels = int(math.log2(C))
    assert 1 << levels == C

    def coupling(l):
        return jnp.logical_and((ii >> (l + 1)) == (jj >> (l + 1)), (ii >> l) != (jj >> l))

    zero_b = jnp.zeros((C, C), BF16)
    Tm = [eye - jnp.where(coupling(0), a, zero_b).astype(F32) for a in Ab]
    for l in range(1, levels):
        m = coupling(l)
        Tb = [t.astype(BF16) for t in Tm]
        G = [jnp.dot(jnp.where(m, a, zero_b), tb, preferred_element_type=F32).astype(BF16) for a, tb in zip(Ab, Tb)]
        Tm = [t - jnp.dot(tb, g, preferred_element_type=F32) for t, tb, g in zip(Tm, Tb, G)]
    sol = []
    for hh in heads:
        vh = v_ref[:, hh * GDN_DV:(hh + 1) * GDN_DV].astype(F32)
        rhs = jnp.concatenate([vh * bcol[hh], kf[hh // 2] * (bcol[hh] * ecol[hh])], axis=-1).astype(BF16)
        sol.append(jnp.dot(Tm[hh].astype(BF16), rhs, preferred_element_type=F32))
    vnb = [(sol[hh][:, :GDN_DV] - jnp.dot(sol[hh][:, GDN_DV:].astype(BF16), Sb[hh], preferred_element_type=F32)
            ).astype(BF16) for hh in heads]
    outs = [jnp.dot((qf[hh // 2] * ecol[hh]).astype(BF16), Sb[hh], preferred_element_type=F32)
            + jnp.dot((qkt[hh // 2] * L[hh]).astype(BF16), vnb[hh], preferred_element_type=F32) for hh in heads]
    s_new = []
    for hh in heads:
        k_tail = (kf[hh // 2] * jnp.exp(dend[hh] - dcol[hh])).astype(BF16)
        s_new.append(St[hh] * jnp.exp(dend[hh]) + lax.dot_general(
            k_tail, vnb[hh], (((0,), (0,)), ((), ())), preferred_element_type=F32))
    for hh in heads:
        s_ref[hh] = s_new[hh]
    for hh in heads:
        sl = slice(hh * GDN_DV, (hh + 1) * GDN_DV)
        o = outs[hh]
        if final:
            o = o + ob_ref[:, sl]
            y = o * lax.rsqrt(jnp.mean(o * o, axis=-1, keepdims=True) + NORM_EPS) * nw_ref[...]
            z = z_ref[:, sl].astype(F32)
            o_ref[:, sl] = (y * (z * jax.nn.sigmoid(z))).astype(o_ref.dtype)
        else:
            o_ref[:, sl] = o


def _gdn_side_inputs(ba, a_log, dt_bias, B, S, C, d):
    T = B * S
    N = S // C
    Hv, HB = GDN_V_HEADS, GDN_HB
    nhb = Hv // HB
    bl = ba[:, d * Hv:(d + 1) * Hv].reshape(T, nhb, HB)
    al = ba[:, 2 * Hv + d * Hv:2 * Hv + (d + 1) * Hv].reshape(T, nhb, HB)
    col = jnp.concatenate([bl, al], axis=-1).transpose(1, 0, 2)
    row = col.reshape(nhb, B * N, C, 2 * HB).transpose(0, 1, 3, 2)
    zeros = jnp.zeros((nhb, HB), F32)
    p0 = jnp.concatenate([zeros, a_log[d].astype(F32).reshape(nhb, HB)], axis=-1)
    p1 = jnp.concatenate([zeros, dt_bias[d].astype(F32).reshape(nhb, HB)], axis=-1)
    pcol = jnp.stack([p0, p1], axis=1)
    prow = jnp.stack([p0, p1], axis=2)
    return col, row, pcol, prow


def _gdn_direction(qk_c, v_c, ba, a_log, dt_bias, B, S, d, final_inputs=None):
    C = min(GDN_CHUNK, S)
    N = S // C
    T = B * S
    HB = GDN_HB
    nhb = GDN_V_HEADS // HB
    reverse = d == 1
    final = final_inputs is not None
    col, row, pcol, prow = _gdn_side_inputs(ba, a_log, dt_bias, B, S, C, d)
    qw = (HB // 2) * GDN_DK
    vw = HB * GDN_DV
    k_off = GDN_QK // qw

    def ch(n):
        return (N - 1 - n) if reverse else n

    in_specs = [
        pl.BlockSpec((C, qw), lambda b, hb, n: (b * N + ch(n), hb)),
        pl.BlockSpec((C, qw), lambda b, hb, n: (b * N + ch(n), k_off + hb)),
        pl.BlockSpec((C, vw), lambda b, hb, n: (b * N + ch(n), hb)),
        pl.BlockSpec((1, C, 2 * HB), lambda b, hb, n: (hb, b * N + ch(n), 0)),
        pl.BlockSpec((1, 1, 2 * HB, C), lambda b, hb, n: (hb, b * N + ch(n), 0, 0)),
        pl.BlockSpec((1, 2, 2 * HB), lambda b, hb, n: (hb, 0, 0)),
        pl.BlockSpec((1, 2 * HB, 2), lambda b, hb, n: (hb, 0, 0)),
    ]
    args = [qk_c, qk_c, v_c, col, row, pcol, prow]
    if final:
        o_other, proj, z_col0, norm_w = final_inputs
        zb = z_col0 // vw
        in_specs += [
            pl.BlockSpec((C, vw), lambda b, hb, n: (b * N + ch(n), hb)),
            pl.BlockSpec((C, vw), lambda b, hb, n: (b * N + ch(n), zb + hb)),
            pl.BlockSpec((1, GDN_DV), lambda b, hb, n: (0, 0)),
        ]
        args += [o_other, proj, norm_w.reshape(1, GDN_DV).astype(F32)]
    return pl.pallas_call(
        functools.partial(_gdn_body, reverse=reverse, final=final),
        grid=(B, nhb, N),
        in_specs=in_specs,
        out_specs=pl.BlockSpec((C, vw), lambda b, hb, n: (b * N + ch(n), hb)),
        out_shape=jax.ShapeDtypeStruct((T, GDN_V), BF16 if final else F32),
        scratch_shapes=[pltpu.VMEM((HB, GDN_DK, GDN_DV), F32)],
        compiler_params=_cparams("parallel", "parallel", "arbitrary"),
        name="gdn_final" if final else "gdn_first",
    )(*args)


def _merge_body(a0_ref, a1_ref, w0_ref, w1_ref, g0_ref, g1_ref, o_ref):
    p0 = jnp.dot(a0_ref[...], w0_ref[0], preferred_element_type=F32)
    p1 = jnp.dot(a1_ref[...], w1_ref[0], preferred_element_type=F32)
    g0 = jax.nn.sigmoid(g0_ref[...].astype(F32))
    g1 = jax.nn.sigmoid(g1_ref[...].astype(F32))
    o_ref[...] = (g0 * p0 + g1 * p1).astype(o_ref.dtype)


def _merge(ret, gdn, w_branch, gate_logits, tm=512, tn=512):
    T, K = ret.shape
    D = w_branch.shape[-1]
    tm = min(tm, T)
    nj = D // tn
    return pl.pallas_call(
        _merge_body,
        grid=(T // tm, nj),
        in_specs=[
            pl.BlockSpec((tm, K), lambda i, j: (i, 0)),
            pl.BlockSpec((tm, K), lambda i, j: (i, 0)),
            pl.BlockSpec((1, K, tn), lambda i, j: (0, 0, j)),
            pl.BlockSpec((1, K, tn), lambda i, j: (1, 0, j)),
            pl.BlockSpec((tm, tn), lambda i, j: (i, j)),
            pl.BlockSpec((tm, tn), lambda i, j: (i, nj + j)),
        ],
        out_specs=pl.BlockSpec((tm, tn), lambda i, j: (i, j)),
        out_shape=jax.ShapeDtypeStruct((T, D), BF16),
        compiler_params=_cparams("parallel", "arbitrary"),
        name="branch_merge",
    )(ret, gdn, w_branch, w_branch, gate_logits, gate_logits)


def _xattn_body(q_ref, k_ref, v_ref, o_ref):
    scale = XA_DH ** -0.5
    for h in range(XA_HEADS):
        sl = slice(h * XA_DH, (h + 1) * XA_DH)
        s = lax.dot_general(q_ref[:, sl], k_ref[:, sl], (((1,), (1,)), ((), ())),
                            preferred_element_type=F32) * scale
        m = jnp.max(s, axis=-1, keepdims=True)
        e = jnp.exp(s - m)
        p = e / jnp.sum(e, axis=-1, keepdims=True)
        o_ref[:, sl] = jnp.dot(p.astype(BF16), v_ref[:, sl], preferred_element_type=F32).astype(o_ref.dtype)


def _xattn(q, kv, B, S, M, tq=512):
    T, D = q.shape
    tq = min(tq, S)
    nq = S // tq
    return pl.pallas_call(
        _xattn_body,
        grid=(B, nq),
        in_specs=[
            pl.BlockSpec((tq, D), lambda b, i: (b * nq + i, 0)),
            pl.BlockSpec((M, D), lambda b, i: (b, 0)),
            pl.BlockSpec((M, D), lambda b, i: (b, 1)),
        ],
        out_specs=pl.BlockSpec((tq, D), lambda b, i: (b * nq + i, 0)),
        out_shape=jax.ShapeDtypeStruct((T, D), BF16),
        compiler_params=_cparams("parallel", "arbitrary"),
        name="cross_attention",
    )(q, kv, kv)


META_IDX, META_GATE, META_RANK = 0, TOP_K, 2 * TOP_K


def _router_body(x_ref, g_ref, w_ref, b_ref, h_ref, meta_ref, cnt_ref, carry_ref):
    i = pl.program_id(0)
    tm = x_ref.shape[0]

    @pl.when(i == 0)
    def _():
        carry_ref[...] = jnp.zeros_like(carry_ref)

    x = x_ref[...]
    h = x * lax.rsqrt(jnp.mean(x * x, axis=-1, keepdims=True) + NORM_EPS) * g_ref[...]
    h_ref[...] = h
    logits = jnp.dot(h, w_ref[...], precision=lax.Precision.HIGHEST, preferred_element_type=F32) + b_ref[...]
    lane = lax.broadcasted_iota(jnp.int32, (tm, LANES), 1)
    neg = jnp.float32(-jnp.inf)
    logits = jnp.where(lane < N_EXPERTS, logits, neg)
    vals, hots = [], []
    for _ in range(TOP_K):
        m = jnp.max(logits, axis=-1, keepdims=True)
        first = jnp.min(jnp.where(logits == m, lane, LANES), axis=-1, keepdims=True)
        hot = lane == first
        vals.append(m)
        hots.append(hot)
        logits = jnp.where(hot, neg, logits)
    es = [jnp.exp(v - vals[0]) for v in vals]
    den = es[0]
    for e in es[1:]:
        den = den + e
    mh = jnp.zeros((tm, LANES), F32)
    for hot in hots:
        mh = mh + hot.astype(F32)
    ii = lax.broadcasted_iota(jnp.int32, (tm, tm), 0)
    jj = lax.broadcasted_iota(jnp.int32, (tm, tm), 1)
    before = (ii > jj).astype(BF16)
    ranks = jnp.dot(before, mh.astype(BF16), preferred_element_type=F32) + carry_ref[...]
    lane_f = lane.astype(F32)
    meta = jnp.zeros((tm, LANES), F32)
    for kk in range(TOP_K):
        e_k = jnp.sum(jnp.where(hots[kk], lane_f, 0.0), axis=-1, keepdims=True)
        r_k = jnp.sum(jnp.where(hots[kk], ranks, 0.0), axis=-1, keepdims=True)
        meta = meta + jnp.where(lane == META_IDX + kk, e_k, 0.0)
        meta = meta + jnp.where(lane == META_GATE + kk, es[kk] / den, 0.0)
        meta = meta + jnp.where(lane == META_RANK + kk, r_k, 0.0)
    meta_ref[...] = meta
    carry_ref[...] = carry_ref[...] + jnp.sum(mh, axis=0, keepdims=True)
    cnt_ref[...] = carry_ref[...]


def _router(x, gain, w_router, b_router, tm=512):
    T, D = x.shape
    tm = min(tm, T)
    wr = jnp.zeros((D, LANES), F32).at[:, :N_EXPERTS].set(w_router.astype(F32))
    br = jnp.zeros((1, LANES), F32).at[0, :N_EXPERTS].set(b_router.astype(F32))
    return pl.pallas_call(
        _router_body,
        grid=(T // tm,),
        in_specs=[
            pl.BlockSpec((tm, D), lambda i: (i, 0)),
            pl.BlockSpec((1, D), lambda i: (0, 0)),
            pl.BlockSpec((D, LANES), lambda i: (0, 0)),
            pl.BlockSpec((1, LANES), lambda i: (0, 0)),
        ],
        out_specs=[
            pl.BlockSpec((tm, D), lambda i: (i, 0)),
            pl.BlockSpec((tm, LANES), lambda i: (i, 0)),
            pl.BlockSpec((1, LANES), lambda i: (0, 0)),
        ],
        out_shape=[
            jax.ShapeDtypeStruct((T, D), F32),
            jax.ShapeDtypeStruct((T, LANES), F32),
            jax.ShapeDtypeStruct((1, LANES), F32),
        ],
        scratch_shapes=[pltpu.VMEM((1, LANES), F32)],
        compiler_params=_cparams("arbitrary"),
        name="moe_router",
    )(x, gain.reshape(1, D).astype(F32), wr, br)


def _dispatch_body(dest_ref, h_ref, init_ref, xs_ref, sem):
    del init_ref
    tm = h_ref.shape[0]

    def row_copy(r, kk):
        return pltpu.make_async_copy(h_ref.at[pl.ds(r, 1)], xs_ref.at[pl.ds(dest_ref[r * TOP_K + kk], 1)], sem)

    def issue(r, carry):
        for kk in range(TOP_K):
            row_copy(r, kk).start()
        return carry

    def drain(r, carry):
        for kk in range(TOP_K):
            row_copy(r, kk).wait()
        return carry

    lax.fori_loop(0, tm, issue, 0)
    lax.fori_loop(0, tm, drain, 0)


def _dispatch(h, dest_flat, n_slots, tm=256):
    T, D = h.shape
    tm = min(tm, T)
    init = jnp.zeros((n_slots, D), F32)
    return pl.pallas_call(
        _dispatch_body,
        grid=(T // tm,),
        in_specs=[
            pl.BlockSpec((tm * TOP_K,), lambda i: (i,), memory_space=pltpu.SMEM),
            pl.BlockSpec((tm, D), lambda i: (i, 0)),
            pl.BlockSpec(memory_space=pl.ANY),
        ],
        out_specs=pl.BlockSpec(memory_space=pl.ANY),
        out_shape=jax.ShapeDtypeStruct((n_slots, D), F32),
        scratch_shapes=[pltpu.SemaphoreType.DMA(())],
        input_output_aliases={2: 0},
        compiler_params=_cparams("arbitrary"),
        name="moe_dispatch",
    )(dest_flat, h, init)


def _expert_body(be_ref, nb_ref, x_ref, wg_ref, wu_ref, bg_ref, bu_ref, wd_ref, bd_ref, o_ref, xb_ref, act_ref, *, nf):
    blk = pl.program_id(0)
    j = pl.program_id(1)
    tf = wg_ref.shape[2]
    used = blk < nb_ref[0]

    @pl.when(jnp.logical_and(used, j == 0))
    def _():
        xb_ref[...] = x_ref[...].astype(BF16)

    for jj in range(nf):
        @pl.when(jnp.logical_and(used, j == jj))
        def _():
            xb = xb_ref[...]
            g = jnp.dot(xb, wg_ref[0], preferred_element_type=F32) + bg_ref[0]
            u = jnp.dot(xb, wu_ref[0], preferred_element_type=F32) + bu_ref[0]
            g = jnp.minimum(g, SWIGLU_LIMIT)
            u = jnp.clip(u, -SWIGLU_LIMIT, SWIGLU_LIMIT)
            act = g * jax.nn.sigmoid(SWIGLU_ALPHA * g) * (u + 1.0)
            act_ref[:, jj * tf:(jj + 1) * tf] = act.astype(BF16)

    @pl.when(jnp.logical_and(used, j >= nf))
    def _():
        o_ref[...] = jnp.dot(act_ref[...], wd_ref[0], preferred_element_type=F32) + bd_ref[0]

    @pl.when(jnp.logical_and(jnp.logical_not(used), j >= nf))
    def _():
        o_ref[...] = jnp.zeros_like(o_ref)


def _experts(xs, block_e, nb_used, wg, wu, bg, bu, wd, bd, tf=1024, tn=1024):
    n_slots, D = xs.shape
    NB = n_slots // MOE_BLOCK
    nf = D_FF // tf
    nn = D // tn

    def xrow(blk, j, be, nb):
        return (jnp.minimum(blk, nb[0] - 1), 0)

    def fcol(j):
        return jnp.minimum(j, nf - 1)

    def ocol(j):
        return jnp.maximum(j - nf, 0)

    grid_spec = pltpu.PrefetchScalarGridSpec(
        num_scalar_prefetch=2,
        grid=(NB, nf + nn),
        in_specs=[
            pl.BlockSpec((MOE_BLOCK, D), xrow),
            pl.BlockSpec((1, D, tf), lambda blk, j, be, nb: (be[blk], 0, fcol(j))),
            pl.BlockSpec((1, D, tf), lambda blk, j, be, nb: (be[blk], 0, fcol(j))),
            pl.BlockSpec((1, 1, tf), lambda blk, j, be, nb: (be[blk], 0, fcol(j))),
            pl.BlockSpec((1, 1, tf), lambda blk, j, be, nb: (be[blk], 0, fcol(j))),
            pl.BlockSpec((1, D_FF, tn), lambda blk, j, be, nb: (be[blk], 0, ocol(j))),
            pl.BlockSpec((1, 1, tn), lambda blk, j, be, nb: (be[blk], 0, ocol(j))),
        ],
        out_specs=pl.BlockSpec((MOE_BLOCK, tn), lambda blk, j, be, nb: (blk, ocol(j))),
        scratch_shapes=[pltpu.VMEM((MOE_BLOCK, D), BF16), pltpu.VMEM((MOE_BLOCK, D_FF), BF16)],
    )
    return pl.pallas_call(
        functools.partial(_expert_body, nf=nf),
        grid_spec=grid_spec,
        out_shape=jax.ShapeDtypeStruct((n_slots, D), F32),
        compiler_params=_cparams("arbitrary", "arbitrary"),
        name="moe_experts",
    )(block_e, nb_used, xs, wg, wu, bg, bu, wd, bd)


def _combine_body(dest_ref, ys_ref, meta_ref, x_ref, g_ref, o_ref, buf_ref, sem):
    tm = x_ref.shape[0]

    def row_copy(r, kk):
        return pltpu.make_async_copy(ys_ref.at[pl.ds(dest_ref[r * TOP_K + kk], 1)],
                                     buf_ref.at[kk, pl.ds(r, 1)], sem)

    def issue(r, carry):
        for kk in range(TOP_K):
            row_copy(r, kk).start()
        return carry

    def drain(r, carry):
        for kk in range(TOP_K):
            row_copy(r, kk).wait()
        return carry

    lax.fori_loop(0, tm, issue, 0)
    lax.fori_loop(0, tm, drain, 0)
    meta = meta_ref[...]
    y = x_ref[...]
    for kk in range(TOP_K):
        y = y + buf_ref[kk] * meta[:, META_GATE + kk:META_GATE + kk + 1]
    o_ref[...] = y * lax.rsqrt(jnp.mean(y * y, axis=-1, keepdims=True) + NORM_EPS) * g_ref[...]


def _combine(ys, dest_flat, meta, x, gain, tm=256):
    T, D = x.shape
    tm = min(tm, T)
    return pl.pallas_call(
        _combine_body,
        grid=(T // tm,),
        in_specs=[
            pl.BlockSpec((tm * TOP_K,), lambda i: (i,), memory_space=pltpu.SMEM),
            pl.BlockSpec(memory_space=pl.ANY),
            pl.BlockSpec((tm, LANES), lambda i: (i, 0)),
            pl.BlockSpec((tm, D), lambda i: (i, 0)),
            pl.BlockSpec((1, D), lambda i: (0, 0)),
        ],
        out_specs=pl.BlockSpec((tm, D), lambda i: (i, 0)),
        out_shape=jax.ShapeDtypeStruct((T, D), F32),
        scratch_shapes=[pltpu.VMEM((TOP_K, tm, D), F32), pltpu.SemaphoreType.DMA(())],
        compiler_params=_cparams("arbitrary"),
        name="moe_combine",
    )(dest_flat, ys, meta, x, gain.reshape(1, D).astype(F32))


def _moe_final(x2, ln_moe, w_router, b_router, ew, ln_final):
    T, D = x2.shape
    A = T * TOP_K
    NB = A // MOE_BLOCK + N_EXPERTS
    h, meta, cnt = _router(x2, ln_moe, w_router, b_router)
    idx = meta[:, META_IDX:META_IDX + TOP_K].astype(jnp.int32)
    rank = meta[:, META_RANK:META_RANK + TOP_K].astype(jnp.int32)
    counts = cnt[0, :N_EXPERTS].astype(jnp.int32)
    padded = (counts + MOE_BLOCK - 1) // MOE_BLOCK * MOE_BLOCK
    pad_end = jnp.cumsum(padded)
    pad_start = pad_end - padded
    dest = (pad_start[idx] + rank).reshape(A)
    block_e = jnp.minimum(jnp.searchsorted(pad_end, jnp.arange(NB, dtype=jnp.int32) * MOE_BLOCK, side='right'),
                          N_EXPERTS - 1).astype(jnp.int32)
    nb_used = (pad_end[-1:] // MOE_BLOCK).astype(jnp.int32)
    xs = _dispatch(h, dest, NB * MOE_BLOCK)
    ys = _experts(xs, block_e, nb_used, *ew)
    return _combine(ys, dest, meta, x2, ln_final)


def _prepare_weights(w_in, w_branch, w_out, xa_w_q, xa_w_kv, xa_w_o, w_gate_up, b_gate_up, w_down, b_down):
    main_w = 2 * RET_QK + 2 * RET_V + 2 * GDN_QK + 2 * GDN_V
    ba_w = 4 * GDN_V_HEADS
    w = {}
    w['in_main'] = w_in[:, :main_w].astype(BF16)
    w['in_ba'] = w_in[:, main_w:main_w + ba_w].astype(BF16)
    w['in_gate'] = w_in[:, main_w + ba_w:].astype(BF16)
    w['branch'] = w_branch.astype(BF16)
    w['out'] = w_out.astype(BF16)
    w['xa_q'] = xa_w_q.astype(BF16)
    w['xa_kv'] = xa_w_kv.astype(BF16)
    w['xa_o'] = xa_w_o.astype(BF16)
    E = w_gate_up.shape[0]
    gu = w_gate_up.reshape(E, D_MODEL, D_FF, 2)
    bgu = b_gate_up.reshape(E, 1, D_FF, 2).astype(F32)
    w['experts'] = (gu[..., 0].astype(BF16), gu[..., 1].astype(BF16), bgu[..., 0], bgu[..., 1],
                    w_down.astype(BF16), b_down.reshape(E, 1, D_MODEL).astype(F32))
    return w


def _rope_tables(S):
    inv_freq = ROPE_BASE ** (-jnp.arange(0, RET_DK, 2, dtype=F32) / RET_DK)
    ang = jnp.arange(S, dtype=F32)[:, None] * inv_freq[None, :]
    return jnp.cos(ang), jnp.sin(ang)


def _encoder(x, mem, w, p):
    B, S, D = x.shape
    M = mem.shape[1]
    T = B * S
    x0 = x.reshape(T, D)
    h = _rmsnorm(x0, p['ln_mix'], BF16)
    proj = _matmul(h, w['in_main'], BF16, tm=2048, name="in_proj")
    ba = _matmul(h, w['in_ba'], F32, name="in_proj_ba")
    gate_logits = _matmul(h, w['in_gate'], BF16, tm=2048, name="in_proj_gate")
    cos, sin = _rope_tables(S)
    ret = _retention(proj, B, S, p['ret_log_decay'], p['ret_gn_w'], cos, sin)
    g0 = 2 * RET_QK + 2 * RET_V
    qk_c = _gdn_conv(proj, B, S, g0, 2 * GDN_QK, p['gdn_conv_w'][:, :2 * GDN_QK], True)
    v_c = _gdn_conv(proj, B, S, g0 + 2 * GDN_QK, GDN_V, p['gdn_conv_w'][:, 2 * GDN_QK:], False)
    o_b = _gdn_direction(qk_c, v_c, ba, p['gdn_a_log'], p['gdn_dt_bias'], B, S, 1)
    gdn = _gdn_direction(qk_c, v_c, ba, p['gdn_a_log'], p['gdn_dt_bias'], B, S, 0,
                         final_inputs=(o_b, proj, g0 + 2 * GDN_QK + GDN_V, p['gdn_norm_w']))
    merged = _merge(ret, gdn, w['branch'], gate_logits)
    x1 = _matmul(merged, w['out'], F32, residual=x0, name="mixer_out")
    hq = _rmsnorm(x1, p['ln_xa'], BF16)
    hm = _rmsnorm(mem.reshape(B * M, D), p['ln_mem'], BF16)
    q = _matmul(hq, w['xa_q'], BF16, name="xa_q")
    kv = _matmul(hm, w['xa_kv'], BF16, name="xa_kv")
    att = _xattn(q, kv, B, S, M)
    x2 = _matmul(att, w['xa_o'], F32, residual=x1, name="xa_out")
    y = _moe_final(x2, p['ln_moe'], p['w_router'], p['b_router'], w['experts'], p['ln_final'])
    return y.reshape(B, S, D)


def kernel(x_prompt, x_sample, mem_prompt, mem_sample, ln_mix, w_in, ret_log_decay, ret_gn_w, gdn_conv_w, gdn_a_log, gdn_dt_bias, gdn_norm_w, w_branch, w_out, ln_xa, ln_mem, xa_w_q, xa_w_kv, xa_w_o, ln_moe, w_router, b_router, w_gate_up, b_gate_up, w_down, b_down, ln_final):
    depth = w_in.shape[0]
    assert depth == 1, "single-layer trunk"
    l = 0
    w = _prepare_weights(w_in[l], w_branch[l], w_out[l], xa_w_q[l], xa_w_kv[l], xa_w_o[l],
                         w_gate_up[l], b_gate_up[l], w_down[l], b_down[l])
    p = dict(ln_mix=ln_mix[l], ret_log_decay=ret_log_decay[l], ret_gn_w=ret_gn_w[l], gdn_conv_w=gdn_conv_w[l],
             gdn_a_log=gdn_a_log[l], gdn_dt_bias=gdn_dt_bias[l], gdn_norm_w=gdn_norm_w[l], ln_xa=ln_xa[l],
             ln_mem=ln_mem[l], ln_moe=ln_moe[l], w_router=w_router[l], b_router=b_router[l], ln_final=ln_final)
    y_prompt = _encoder(x_prompt, mem_prompt, w, p)
    y_sample = _encoder(x_sample, mem_sample, w, p)
    return (y_prompt, y_sample)
```
